```python
import jax, jax.numpy as jnp
from jax import lax
import numpy as np

D_MODEL = 1024
BATCH = 8
SEQ = 2048
DEPTH = 1
DEC_BATCH = 128
DEC_SEQ = 1
PAST_LEN = 8192
PAGE_SIZE = 128

HEAD_DIM = 64
GROUPS = ((128, 1), (512, 4), (2048, 16))
H_G = 4
N_HEADS = H_G * len(GROUPS)
ATTN_WIDTH = N_HEADS * HEAD_DIM
ATTN_OUT = H_G * HEAD_DIM
ROT_DIM = HEAD_DIM // 4
ROPE_THETA = 500000.0
C_CONV = D_MODEL
CONV_WIDTH = 31
D_FF = -(-8 * D_MODEL // (3 * 256)) * 256
PLE_DIM = 256
NORM_EPS = 1e-6
SPLITS = [2 * C_CONV, ATTN_WIDTH, ATTN_WIDTH, ATTN_WIDTH, 2 * D_MODEL]
IN_COLS = sum(SPLITS)

kernel_name = "gated_conformer_dilated_attn_decoder_step"


def rms_norm(x, g):
    xf = x.astype(jnp.float32)
    y = xf * lax.rsqrt(jnp.mean(xf * xf, axis=-1, keepdims=True) + NORM_EPS)
    return (y * g.astype(jnp.float32)).astype(x.dtype)


def layer_norm(x, g, b):
    xf = x.astype(jnp.float32)
    mu = jnp.mean(xf, axis=-1, keepdims=True)
    var = jnp.mean(jnp.square(xf - mu), axis=-1, keepdims=True)
    y = (xf - mu) * lax.rsqrt(var + NORM_EPS)
    return (y * g.astype(jnp.float32) + b.astype(jnp.float32)).astype(x.dtype)


def rope_partial(x, pos):
    half = ROT_DIM // 2
    inv_freq = ROPE_THETA ** (-jnp.arange(half, dtype=jnp.float32) / half)
    ang = pos.astype(jnp.float32)[:, None] * inv_freq[None, :]
    cos = jnp.cos(ang)[:, None, :]
    sin = jnp.sin(ang)[:, None, :]
    xr = x[..., :ROT_DIM].astype(jnp.float32)
    x1, x2 = xr[..., :half], xr[..., half:]
    rot = jnp.concatenate([x1 * cos - x2 * sin, x2 * cos + x1 * sin], axis=-1)
    return jnp.concatenate([rot.astype(x.dtype), x[..., ROT_DIM:]], axis=-1)


def causal_dwconv(u, prev, w, b):
    full = jnp.concatenate([prev, u], axis=1)
    out = lax.conv_general_dilated(full, w[:, None, :].astype(full.dtype), window_strides=(1,),
                                   padding='VALID', dimension_numbers=('NWC', 'WIO', 'NWC'),
                                   feature_group_count=C_CONV)
    return out + b, full[:, full.shape[1] - (CONV_WIDTH - 1):]


def dilated_group_prompt(q, k, v, window, dil):
    B, S, H, Dh = q.shape
    nk = window // dil
    span = nk * dil
    s_pad = -(-S // span) * span
    M = s_pad // dil
    nb = M // nk

    def to_blocks(t):
        t = jnp.pad(t, ((0, 0), (0, s_pad - S), (0, 0), (0, 0)))
        t = t.reshape(B, M, dil, H, Dh).transpose(0, 2, 1, 3, 4)
        return t.reshape(B, dil, nb, nk, H, Dh)

    def with_prev(t):
        prev = jnp.pad(t[:, :, :-1], ((0, 0), (0, 0), (1, 0), (0, 0), (0, 0), (0, 0)))
        return jnp.concatenate([prev, t], axis=3)

    qb = to_blocks(q)
    kb = with_prev(to_blocks(k))
    vb = with_prev(to_blocks(v))
    s = jnp.einsum('brnqhd,brnkhd->brnhqk', qb, kb,
                   preferred_element_type=jnp.float32) * (HEAD_DIM ** -0.5)
    qi = jnp.arange(nk)[:, None]
    kj = jnp.arange(2 * nk)[None, :]
    dist = nk + qi - kj
    band = (dist >= 0) & (dist <= nk)
    blk_ok = (jnp.arange(nb)[:, None, None] > 0) | (kj >= nk)[None]
    mask = band[None] & blk_ok
    s = jnp.where(mask[:, None], s, -jnp.inf)
    m = jnp.max(s, axis=-1, keepdims=True)
    p = jnp.exp(s - m)
    l = jnp.sum(p, axis=-1)
    o = jnp.einsum('brnhqk,brnkhd->brnqhd', p, vb.astype(jnp.float32))
    o = o / jnp.swapaxes(l, -1, -2)[..., None]
    lse = jnp.swapaxes(m[..., 0] + jnp.log(l), -1, -2)

    def from_blocks(t):
        t = t.reshape((B, dil, M) + t.shape[4:])
        t = jnp.swapaxes(t, 1, 2)
        return t.reshape((B, s_pad) + t.shape[3:])[:, :S]

    return from_blocks(o), from_blocks(lse)


def dilated_group_sample(q, k, v, buf, window, dil):
    L = buf.shape[1]
    T = q.shape[1]
    nk = window // dil
    kv_all = jnp.concatenate([buf, jnp.stack([k, v], axis=2)], axis=1)
    idx = L + jnp.arange(T)[:, None] - dil * jnp.arange(nk + 1)[None, :]
    valid = idx >= 0
    g = jnp.take(kv_all, jnp.clip(idx, 0), axis=1)
    s = jnp.einsum('bthd,btkhd->bthk', q, g[:, :, :, 0],
                   preferred_element_type=jnp.float32) * (HEAD_DIM ** -0.5)
    s = jnp.where(valid[None, :, None, :], s, -jnp.inf)
    m = jnp.max(s, axis=-1, keepdims=True)
    p = jnp.exp(s - m)
    l = jnp.sum(p, axis=-1)
    o = jnp.einsum('bthk,btkhd->bthd', p, g[:, :, :, 1].astype(jnp.float32)) / l[..., None]
    lse = m[..., 0] + jnp.log(l)
    return o, lse, kv_all[:, kv_all.shape[1] - L:]


def window_tail(k, v, length):
    kv = jnp.stack([k, v], axis=2)
    S = kv.shape[1]
    if S < length:
        kv = jnp.pad(kv, ((0, 0), (length - S, 0), (0, 0), (0, 0), (0, 0)))
    return kv[:, kv.shape[1] - length:]


def hybrid_layer(x, pe, pos, conv_prev, win_bufs, is_prompt, w_in, g_mix, w_dw, b_dw, ln_g, ln_b,
                 w_conv_out, w_attn_out, w_o, g_ffn, w_ffn_in, w_ffn_out, g_ple, w_ple_gate, w_ple_proj):
    B, S, _ = x.shape
    h = rms_norm(x, g_mix)
    z = h @ w_in
    u_pre, q, k, v, gate_logits = jnp.split(z, list(np.cumsum(SPLITS)[:-1]), axis=-1)
    u = u_pre[..., :C_CONV] * jax.nn.sigmoid(u_pre[..., C_CONV:])
    c, conv_new = causal_dwconv(u, conv_prev, w_dw, b_dw)
    a_out = jax.nn.silu(layer_norm(c, ln_g, ln_b)) @ w_conv_out
    q = rope_partial(q.reshape(B, S, N_HEADS, HEAD_DIM), pos)
    k = rope_partial(k.reshape(B, S, N_HEADS, HEAD_DIM), pos)
    v = v.reshape(B, S, N_HEADS, HEAD_DIM)
    outs, lses, new_bufs = [], [], []
    for gi, (win, dil) in enumerate(GROUPS):
        hs = slice(gi * H_G, (gi + 1) * H_G)
        qg, kg, vg = q[:, :, hs], k[:, :, hs], v[:, :, hs]
        if is_prompt:
            o, lse = dilated_group_prompt(qg, kg, vg, win, dil)
            nbuf = window_tail(kg, vg, min(win, PAST_LEN))
        else:
            o, lse, nbuf = dilated_group_sample(qg, kg, vg, win_bufs[gi], win, dil)
        outs.append(o)
        lses.append(lse)
        new_bufs.append(nbuf)
    wts = jax.nn.softmax(jnp.stack(lses, axis=0), axis=0)
    o = jnp.sum(wts[..., None] * jnp.stack(outs, axis=0), axis=0)
    b_out = o.astype(x.dtype).reshape(B, S, ATTN_OUT) @ w_attn_out
    gates = jax.nn.sigmoid(gate_logits)
    merged = gates[..., :D_MODEL] * a_out + gates[..., D_MODEL:] * b_out
    x = x + merged @ w_o
    gu = rms_norm(x, g_ffn) @ w_ffn_in
    x = x + (jax.nn.silu(gu[..., :D_FF]) * gu[..., D_FF:]) @ w_ffn_out
    x = x + jax.nn.sigmoid(rms_norm(x, g_ple) @ w_ple_gate) * (pe.astype(x.dtype) @ w_ple_proj)
    return x, conv_new, new_bufs


def setup_inputs(seed: int = 0) -> dict:
    key = jax.random.key(seed)
    ks = jax.random.split(key, 32)
    f32 = jnp.float32
    nrm = lambda k, shape, scale: jax.random.normal(k, shape, f32) * scale
    win_lens = [min(w, PAST_LEN) for w, _ in GROUPS]
    return {
        'x_prompt': nrm(ks[0], (BATCH, SEQ, D_MODEL), 1.0),
        'x_sample': nrm(ks[1], (DEC_BATCH, DEC_SEQ, D_MODEL), 1.0),
        'state_conv': nrm(ks[2], (DEPTH, DEC_BATCH, CONV_WIDTH - 1, C_CONV), 0.5),
        'cache_win_a': nrm(ks[3], (DEPTH, DEC_BATCH, win_lens[0], 2, H_G, HEAD_DIM), 1.0),
        'cache_win_b': nrm(ks[4], (DEPTH, DEC_BATCH, win_lens[1], 2, H_G, HEAD_DIM), 1.0),
        'cache_win_c': nrm(ks[5], (DEPTH, DEC_BATCH, win_lens[2], 2, H_G, HEAD_DIM), 1.0),
        'p_prompt': nrm(ks[6], (DEPTH, BATCH, SEQ, PLE_DIM), 1.0),
        'p_sample': nrm(ks[7], (DEPTH, DEC_BATCH, DEC_SEQ, PLE_DIM), 1.0),
        'w_in': nrm(ks[8], (DEPTH, D_MODEL, IN_COLS), D_MODEL ** -0.5),
        'g_mix': 1.0 + nrm(ks[9], (DEPTH, D_MODEL), 0.01),
        'w_dw': nrm(ks[10], (DEPTH, CONV_WIDTH, C_CONV), CONV_WIDTH ** -0.5),
        'b_dw': nrm(ks[11], (DEPTH, C_CONV), 0.01),
        'ln_g': 1.0 + nrm(ks[12], (DEPTH, C_CONV), 0.01),
        'ln_b': nrm(ks[13], (DEPTH, C_CONV), 0.01),
        'w_conv_out': nrm(ks[14], (DEPTH, C_CONV, D_MODEL), C_CONV ** -0.5),
        'w_attn_out': nrm(ks[15], (DEPTH, ATTN_OUT, D_MODEL), ATTN_OUT ** -0.5),
        'w_o': nrm(ks[16], (DEPTH, D_MODEL, D_MODEL), D_MODEL ** -0.5),
        'g_ffn': 1.0 + nrm(ks[17], (DEPTH, D_MODEL), 0.01),
        'w_ffn_in': nrm(ks[18], (DEPTH, D_MODEL, 2 * D_FF), D_MODEL ** -0.5),
        'w_ffn_out': nrm(ks[19], (DEPTH, D_FF, D_MODEL), D_FF ** -0.5),
        'g_ple': 1.0 + nrm(ks[20], (DEPTH, D_MODEL), 0.01),
        'w_ple_gate': nrm(ks[21], (DEPTH, D_MODEL, D_MODEL), D_MODEL ** -0.5),
        'w_ple_proj': nrm(ks[22], (DEPTH, PLE_DIM, D_MODEL), PLE_DIM ** -0.5),
        'g_final': 1.0 + nrm(ks[23], (D_MODEL,), 0.01),
    }


def reference(x_prompt, x_sample, state_conv, cache_win_a, cache_win_b, cache_win_c, p_prompt, p_sample,
              w_in, g_mix, w_dw, b_dw, ln_g, ln_b, w_conv_out, w_attn_out, w_o, g_ffn, w_ffn_in, w_ffn_out,
              g_ple, w_ple_gate, w_ple_proj, g_final):
    pos_p = jnp.arange(x_prompt.shape[1], dtype=jnp.int32)
    pos_s = PAST_LEN + jnp.arange(x_sample.shape[1], dtype=jnp.int32)
    xp, xs = x_prompt, x_sample
    conv_p, conv_s = [], []
    win_p, win_s = [[], [], []], [[], [], []]
    for i in range(DEPTH):
        lw = (w_in[i], g_mix[i], w_dw[i], b_dw[i], ln_g[i], ln_b[i], w_conv_out[i], w_attn_out[i], w_o[i],
              g_ffn[i], w_ffn_in[i], w_ffn_out[i], g_ple[i], w_ple_gate[i], w_ple_proj[i])
        zeros_ctx = jnp.zeros((xp.shape[0], CONV_WIDTH - 1, C_CONV), xp.dtype)
        xp, cp, bp = hybrid_layer(xp, p_prompt[i], pos_p, zeros_ctx, None, True, *lw)
        xs, cs, bs = hybrid_layer(xs, p_sample[i], pos_s, state_conv[i],
                                  (cache_win_a[i], cache_win_b[i], cache_win_c[i]), False, *lw)
        conv_p.append(cp)
        conv_s.append(cs)
        for gi in range(len(GROUPS)):
            win_p[gi].append(bp[gi])
            win_s[gi].append(bs[gi])
    y_prompt = rms_norm(xp, g_final)
    y_sample = rms_norm(xs, g_final)
    new_conv_prompt = jnp.stack(conv_p, axis=0)
    new_win_a_prompt = jnp.stack(win_p[0], axis=0)
    new_win_b_prompt = jnp.stack(win_p[1], axis=0)
    new_win_c_prompt = jnp.stack(win_p[2], axis=0)
    new_conv_sample = jnp.stack(conv_s, axis=0)
    new_win_a_sample = jnp.stack(win_s[0], axis=0)
    new_win_b_sample = jnp.stack(win_s[1], axis=0)
    new_win_c_sample = jnp.stack(win_s[2], axis=0)
    return (y_prompt, y_sample, new_conv_prompt, new_win_a_prompt, new_win_b_prompt, new_win_c_prompt,
            new_conv_sample, new_win_a_sample, new_win_b_sample, new_win_c_sample)
```

```python
import functools

import jax
import jax.numpy as jnp
from jax import lax
from jax.experimental import pallas as pl
from jax.experimental.pallas import tpu as pltpu

D_MODEL = 1024
BATCH = 8
SEQ = 2048
DEC_BATCH = 128
PAST_LEN = 8192
HEAD_DIM = 64
GROUPS = ((128, 1), (512, 4), (2048, 16))
H_G = 4
N_GROUPS = len(GROUPS)
GROUP_WIDTH = H_G * HEAD_DIM
ATTN_WIDTH = N_GROUPS * GROUP_WIDTH
ROT_DIM = HEAD_DIM // 4
ROPE_THETA = 500000.0
C_CONV = D_MODEL
CONV_WIDTH = 31
D_FF = 2816
PLE_DIM = 256
NORM_EPS = 1e-6
NK = 128
Q_OFF = 2 * C_CONV
K_OFF = Q_OFF + ATTN_WIDTH
V_OFF = K_OFF + ATTN_WIDTH
GATE_OFF = V_OFF + ATTN_WIDTH
IN_COLS = GATE_OFF + 2 * D_MODEL

F32 = jnp.float32
BF16 = jnp.bfloat16
LANES = 128
MASKED = -1e30
VMEM_LIMIT = 52 * 1024 * 1024


def _const_spec(shape):
    return pl.BlockSpec(shape, lambda *_: (0,) * len(shape), pipeline_mode=pl.Buffered(1))


def _params(n_axes):
    return pltpu.CompilerParams(dimension_semantics=("arbitrary",) * n_axes,
                                vmem_limit_bytes=VMEM_LIMIT)


def _rms(x, g):
    return x * lax.rsqrt(jnp.mean(x * x, axis=-1, keepdims=True) + NORM_EPS) * g


def _in_proj_kernel(x_ref, g_ref, w_ref, cos_ref, sa_ref, sb_ref,
                    u_ref, q_ref, kva_ref, kvb_ref, kvc_ref, gate_ref):
    hb = _rms(x_ref[...], g_ref[...]).astype(BF16)
    ch = GROUP_WIDTH

    def mm(c0):
        return jnp.dot(hb, w_ref[:, c0:c0 + ch], preferred_element_type=F32)

    for c in range(0, C_CONV, ch):
        u_ref[:, c:c + ch] = mm(c) * jax.nn.sigmoid(mm(C_CONV + c))

    cos = cos_ref[...]
    sa = sa_ref[...]
    sb = sb_ref[...]

    def rope(z):
        return z * cos + pltpu.roll(z, ch - ROT_DIM // 2, 1) * sa + pltpu.roll(z, ROT_DIM // 2, 1) * sb

    kv_refs = (kva_ref, kvb_ref, kvc_ref)
    for g in range(N_GROUPS):
        c = g * ch
        q_ref[:, c:c + ch] = rope(mm(Q_OFF + c)) * (HEAD_DIM ** -0.5)
        kv_refs[g][:, 0:ch] = rope(mm(K_OFF + c))
        kv_refs[g][:, ch:2 * ch] = mm(V_OFF + c)

    for c in range(0, 2 * D_MODEL, ch):
        gate_ref[:, c:c + ch] = jax.nn.sigmoid(mm(GATE_OFF + c)).astype(BF16)


def _in_proj(x2d, g, w, cos, sa, sb, tm):
    rows = x2d.shape[0]
    tab_blocks = cos.shape[0] // tm
    row = lambda n: pl.BlockSpec((tm, n), lambda i: (i, 0))
    tab = pl.BlockSpec((tm, GROUP_WIDTH), lambda i: (i % tab_blocks, 0))
    return pl.pallas_call(
        _in_proj_kernel,
        grid=(rows // tm,),
        in_specs=[row(D_MODEL), _const_spec((1, D_MODEL)), _const_spec((D_MODEL, IN_COLS)), tab, tab, tab],
        out_specs=[row(C_CONV), row(ATTN_WIDTH), row(2 * GROUP_WIDTH), row(2 * GROUP_WIDTH),
                   row(2 * GROUP_WIDTH), row(2 * D_MODEL)],
        out_shape=[jax.ShapeDtypeStruct((rows, C_CONV), F32),
                   jax.ShapeDtypeStruct((rows, ATTN_WIDTH), F32),
                   jax.ShapeDtypeStruct((rows, 2 * GROUP_WIDTH), F32),
                   jax.ShapeDtypeStruct((rows, 2 * GROUP_WIDTH), F32),
                   jax.ShapeDtypeStruct((rows, 2 * GROUP_WIDTH), F32),
                   jax.ShapeDtypeStruct((rows, 2 * D_MODEL), BF16)],
        compiler_params=_params(1),
        name="in_proj",
    )(x2d, g, w, cos, sa, sb)


def _combine_groups(lses, outs):
    mx = jnp.maximum(jnp.maximum(lses[0], lses[1]), lses[2])
    es = [jnp.exp(l - mx) for l in lses]
    num = es[0] * outs[0] + es[1] * outs[1] + es[2] * outs[2]
    return num / (es[0] + es[1] + es[2])


def _attn_kernel(qa_ref, qb_ref, qc_ref, ka_ref, va_ref, kb_ref, vb_ref, kc_ref, vc_ref,
                 o_ref, og_ref, lg_ref):
    lane = lax.broadcasted_iota(jnp.int32, (NK, LANES), 1)
    qi = lax.broadcasted_iota(jnp.int32, (NK, LANES), 0)
    first_head = lane < HEAD_DIM
    own_mask = lane <= qi
    prev_mask = lane >= qi
    nt = (((1,), (1,)), ((), ()))

    def block(g, q_ref, k_ref, v_ref, rows, prev_rows):
        qb = q_ref[rows, :]
        ko = k_ref[rows, :].astype(BF16)
        vo = v_ref[rows, :].astype(BF16)
        if prev_rows is not None:
            kp = k_ref[prev_rows, :].astype(BF16)
            vp = v_ref[prev_rows, :].astype(BF16)
        outs, lses = [], []
        for head_mask in (first_head, jnp.logical_not(first_head)):
            qh = jnp.where(head_mask, qb, 0.0).astype(BF16)
            so = jnp.where(own_mask, lax.dot_general(qh, ko, nt, preferred_element_type=F32), MASKED)
            m = jnp.max(so, axis=-1, keepdims=True)
            if prev_rows is not None:
                sp = jnp.where(prev_mask, lax.dot_general(qh, kp, nt, preferred_element_type=F32), MASKED)
                m = jnp.maximum(m, jnp.max(sp, axis=-1, keepdims=True))
            po = jnp.exp(so - m)
            l = jnp.sum(po, axis=-1, keepdims=True)
            acc = jnp.dot(po.astype(BF16), vo, preferred_element_type=F32)
            if prev_rows is not None:
                pp = jnp.exp(sp - m)
                l = l + jnp.sum(pp, axis=-1, keepdims=True)
                acc = acc + jnp.dot(pp.astype(BF16), vp, preferred_element_type=F32)
            outs.append(acc / l)
            lses.append(m + jnp.log(l))
        og_ref[g, rows, :] = jnp.where(first_head, outs[0], outs[1])
        lg_ref[g, rows, :] = jnp.where(first_head, lses[0], lses[1])

    for g, (q_ref, k_ref, v_ref) in enumerate(((qa_ref, ka_ref, va_ref), (qb_ref, kb_ref, vb_ref),
                                               (qc_ref, kc_ref, vc_ref))):
        dil = GROUPS[g][1]
        span = NK * dil
        n_blocks = SEQ // span

        def rows_at(start, dil=dil):
            return pl.ds(start, NK) if dil == 1 else pl.ds(start, NK, stride=dil)

        def residue(r, carry, g=g, q_ref=q_ref, k_ref=k_ref, v_ref=v_ref, span=span,
                    n_blocks=n_blocks, rows_at=rows_at):
            block(g, q_ref, k_ref, v_ref, rows_at(r), None)

            def later(n, c):
                start = r + n * span
                block(g, q_ref, k_ref, v_ref, rows_at(start), rows_at(start - span))
                return c

            if n_blocks > 1:
                lax.fori_loop(1, n_blocks, later, 0)
            return carry

        lax.fori_loop(0, dil, residue, 0)

    tc = 256
    for t in range(0, SEQ, tc):
        rows = slice(t, t + tc)
        o_ref[rows, :] = _combine_groups([lg_ref[g, rows, :] for g in range(N_GROUPS)],
                                         [og_ref[g, rows, :] for g in range(N_GROUPS)]).astype(o_ref.dtype)


def _attention(q3, kva3, kvb3, kvc3):
    def col(c):
        return pl.BlockSpec((None, SEQ, LANES), lambda b, hp, c=c: (b, 0, c + hp))
    n_pairs = GROUP_WIDTH // LANES
    return pl.pallas_call(
        _attn_kernel,
        grid=(BATCH, n_pairs),
        in_specs=[col(0), col(n_pairs), col(2 * n_pairs),
                  col(0), col(n_pairs), col(0), col(n_pairs), col(0), col(n_pairs)],
        out_specs=pl.BlockSpec((None, SEQ, LANES), lambda b, hp: (b, 0, hp)),
        out_shape=jax.ShapeDtypeStruct((BATCH, SEQ, GROUP_WIDTH), BF16),
        scratch_shapes=[pltpu.VMEM((N_GROUPS, SEQ, LANES), F32), pltpu.VMEM((N_GROUPS, SEQ, LANES), F32)],
        compiler_params=_params(2),
        name="prompt_attention",
    )(q3, q3, q3, kva3, kva3, kvb3, kvb3, kvc3, kvc3)


def _merge(c, x, ob, gate_ref, lng_ref, lnb_ref, wco_ref, wao_ref, wo_ref):
    mu = jnp.mean(c, axis=-1, keepdims=True)
    d = c - mu
    var = jnp.mean(d * d, axis=-1, keepdims=True)
    y = d * lax.rsqrt(var + NORM_EPS) * lng_ref[...] + lnb_ref[...]
    a_out = jnp.dot(jax.nn.silu(y).astype(BF16), wco_ref[...], preferred_element_type=F32)
    b_out = jnp.dot(ob, wao_ref[...], preferred_element_type=F32)
    merged = (gate_ref[:, 0:D_MODEL].astype(F32) * a_out
              + gate_ref[:, D_MODEL:2 * D_MODEL].astype(F32) * b_out)
    return x + jnp.dot(merged.astype(BF16), wo_ref[...], preferred_element_type=F32)


CONV_TS = 256
CONV_HALO = 32


def _prompt_mix_kernel(u_ref, x_ref, o_ref, gate_ref, wdw_ref, bdw_ref, lng_ref, lnb_ref,
                       wco_ref, wao_ref, wo_ref, x1_ref, win_ref, c_ref):
    j = pl.program_id(1)
    t0 = pl.multiple_of(j * CONV_TS, CONV_TS)

    @pl.when(j == 0)
    def _():
        win_ref[0:CONV_HALO, :] = jnp.zeros((CONV_HALO, C_CONV), F32)

    @pl.when(j > 0)
    def _():
        win_ref[0:CONV_HALO, :] = u_ref[pl.ds(t0 - CONV_HALO, CONV_HALO), :]

    win_ref[CONV_HALO:CONV_HALO + CONV_TS, :] = u_ref[pl.ds(t0, CONV_TS), :]
    first = CONV_HALO - (CONV_WIDTH - 1)

    def lane_chunk(lc, carry):
        l0 = pl.multiple_of(lc * LANES, LANES)
        acc = jnp.zeros((CONV_TS, LANES), F32)
        for k in range(CONV_WIDTH):
            acc = acc + win_ref[pl.ds(first + k, CONV_TS), pl.ds(l0, LANES)] * wdw_ref[k:k + 1, pl.ds(l0, LANES)]
        c_ref[:, pl.ds(l0, LANES)] = acc + bdw_ref[:, pl.ds(l0, LANES)]
        return carry

    lax.fori_loop(0, C_CONV // LANES, lane_chunk, 0)
    x1_ref[...] = _merge(c_ref[...], x_ref[...], o_ref[...], gate_ref, lng_ref, lnb_ref,
                         wco_ref, wao_ref, wo_ref)


def _prompt_mix(u3, x3, o3, gate3, wdw, bdw, lng, lnb, wco, wao, wo):
    tile = lambda n: pl.BlockSpec((None, CONV_TS, n), lambda b, j: (b, j, 0))
    return pl.pallas_call(
        _prompt_mix_kernel,
        grid=(BATCH, SEQ // CONV_TS),
        in_specs=[pl.BlockSpec((None, SEQ, C_CONV), lambda b, j: (b, 0, 0)),
                  tile(D_MODEL), tile(GROUP_WIDTH), tile(2 * D_MODEL),
                  _const_spec((CONV_WIDTH, C_CONV)), _const_spec((1, C_CONV)),
                  _const_spec((1, C_CONV)), _const_spec((1, C_CONV)),
                  _const_spec((C_CONV, D_MODEL)), _const_spec((GROUP_WIDTH, D_MODEL)),
                  _const_spec((D_MODEL, D_MODEL))],
        out_specs=tile(D_MODEL),
        out_shape=jax.ShapeDtypeStruct((BATCH, SEQ, D_MODEL), F32),
        scratch_shapes=[pltpu.VMEM((CONV_HALO + CONV_TS, C_CONV), F32), pltpu.VMEM((CONV_TS, C_CONV), F32)],
        compiler_params=_params(2),
        name="prompt_mix",
    )(u3, x3, o3, gate3, wdw, bdw, lng, lnb, wco, wao, wo)


def _sample_mix_kernel(c_ref, x_ref, og_ref, lg_ref, gate_ref, lng_ref, lnb_ref, wco_ref, wao_ref, wo_ref,
                       x1_ref):
    ob = _combine_groups([lg_ref[g] for g in range(N_GROUPS)],
                         [og_ref[g] for g in range(N_GROUPS)]).astype(BF16)
    x1_ref[...] = _merge(c_ref[...], x_ref[...], ob, gate_ref, lng_ref, lnb_ref, wco_ref, wao_ref, wo_ref)


def _sample_mix(c, x, og, lg, gate, lng, lnb, wco, wao, wo):
    rows = x.shape[0]
    full = lambda n: _const_spec((rows, n))
    per_group = _const_spec((N_GROUPS, rows, GROUP_WIDTH))
    return pl.pallas_call(
        _sample_mix_kernel,
        grid=(1,),
        in_specs=[full(C_CONV), full(D_MODEL), per_group, per_group, full(2 * D_MODEL),
                  _const_spec((1, C_CONV)), _const_spec((1, C_CONV)),
                  _const_spec((C_CONV, D_MODEL)), _const_spec((GROUP_WIDTH, D_MODEL)),
                  _const_spec((D_MODEL, D_MODEL))],
        out_specs=pl.BlockSpec((rows, D_MODEL), lambda i: (0, 0)),
        out_shape=jax.ShapeDtypeStruct((rows, D_MODEL), F32),
        compiler_params=_params(1),
        name="sample_mix",
    )(c, x, og, lg, gate, lng, lnb, wco, wao, wo)


FF_CHUNK = 256


def _ffn_kernel(x_ref, p_ref, gf_ref, wfi_ref, wfo_ref, gp_ref, wpg_ref, wpp_ref, gfin_ref, y_ref, acc_ref):
    x1 = x_ref[...]
    hb = _rms(x1, gf_ref[...]).astype(BF16)
    acc_ref[...] = jnp.zeros_like(acc_ref)
    for c in range(0, D_FF, FF_CHUNK):
        gch = jnp.dot(hb, wfi_ref[:, c:c + FF_CHUNK], preferred_element_type=F32)
        uch = jnp.dot(hb, wfi_ref[:, D_FF + c:D_FF + c + FF_CHUNK], preferred_element_type=F32)
        act = (jax.nn.silu(gch) * uch).astype(BF16)
        acc_ref[...] += jnp.dot(act, wfo_ref[c:c + FF_CHUNK, :], preferred_element_type=F32)
    x2 = x1 + acc_ref[...]
    hp = _rms(x2, gp_ref[...]).astype(BF16)
    gate = jax.nn.sigmoid(jnp.dot(hp, wpg_ref[...], preferred_element_type=F32))
    pe = jnp.dot(p_ref[...].astype(BF16), wpp_ref[...], preferred_element_type=F32)
    x3 = x2 + gate * pe
    y_ref[...] = _rms(x3, gfin_ref[...])


def _ffn(x1, p, gf, wfi, wfo, gp, wpg, wpp, gfin, tm):
    rows = x1.shape[0]
    row = lambda n: pl.BlockSpec((tm, n), lambda i: (i, 0))
    return pl.pallas_call(
        _ffn_kernel,
        grid=(rows // tm,),
        in_specs=[row(D_MODEL), row(PLE_DIM), _const_spec((1, D_MODEL)),
                  _const_spec((D_MODEL, 2 * D_FF)), _const_spec((D_FF, D_MODEL)),
                  _const_spec((1, D_MODEL)), _const_spec((D_MODEL, D_MODEL)),
                  _const_spec((PLE_DIM, D_MODEL)), _const_spec((1, D_MODEL))],
        out_specs=row(D_MODEL),
        out_shape=jax.ShapeDtypeStruct((rows, D_MODEL), F32),
        scratch_shapes=[pltpu.VMEM((tm, D_MODEL), F32)],
        compiler_params=_params(1),
        name="ffn_ple",
    )(x1, p, gf, wfi, wfo, gp, wpg, wpp, gfin)


SAMPLE_CONV_BLOCK = 32


def _sample_conv_kernel(state_ref, u_ref, wdw_ref, bdw_ref, c_ref, new_ref):
    ctx = CONV_WIDTH - 1
    u = u_ref[...]
    acc = u * wdw_ref[ctx:ctx + 1, :] + bdw_ref[...]
    for k in range(ctx):
        acc = acc + state_ref[k] * wdw_ref[k:k + 1, :]
    c_ref[...] = acc
    for k in range(ctx - 1):
        new_ref[k] = state_ref[k + 1]
    new_ref[ctx - 1] = u


def _sample_conv(state_t, u, wdw, bdw):
    ctx, n, _ = state_t.shape
    sb = SAMPLE_CONV_BLOCK
    return pl.pallas_call(
        _sample_conv_kernel,
        grid=(n // sb,),
        in_specs=[pl.BlockSpec((ctx, sb, C_CONV), lambda i: (0, i, 0)),
                  pl.BlockSpec((sb, C_CONV), lambda i: (i, 0)),
                  _const_spec((CONV_WIDTH, C_CONV)), _const_spec((1, C_CONV))],
        out_specs=[pl.BlockSpec((sb, C_CONV), lambda i: (i, 0)),
                   pl.BlockSpec((ctx, sb, C_CONV), lambda i: (0, i, 0))],
        out_shape=[jax.ShapeDtypeStruct((n, C_CONV), F32),
                   jax.ShapeDtypeStruct((ctx, n, C_CONV), F32)],
        compiler_params=_params(1),
        name="sample_conv",
    )(state_t, u, wdw, bdw)


def _sample_cache_kernel(q_ref, new_ref, cache_ref, out_ref, o_ref, lse_ref, *, dil):
    sb, _, _, _, length = cache_ref.shape
    lane = lax.broadcasted_iota(jnp.int32, (1, length), 1)
    used = (lane & (dil - 1)) == 0
    last = lax.broadcasted_iota(jnp.int32, (HEAD_DIM, length), 1) == length - 1

    def sample(s, carry):
        for h in range(H_G):
            q = q_ref[s, h]
            k_new = new_ref[s, 0, h]
            v_new = new_ref[s, 1, h]
            keys = cache_ref[s, 0, h]
            vals = cache_ref[s, 1, h]
            sc = jnp.where(used, jnp.sum(keys * q, axis=0, keepdims=True), MASKED)
            s_new = jnp.sum(k_new * q, axis=0, keepdims=True)
            m = jnp.maximum(jnp.max(sc, axis=1, keepdims=True), s_new)
            p = jnp.exp(sc - m)
            p_new = jnp.exp(s_new - m)
            l = jnp.sum(p, axis=1, keepdims=True) + p_new
            o_ref[s, h] = (jnp.sum(vals * p, axis=1, keepdims=True) + v_new * p_new) / l
            lse_ref[s, h] = jnp.broadcast_to(m + jnp.log(l), (HEAD_DIM, 1))
            out_ref[s, 0, h] = jnp.where(last, k_new, pltpu.roll(keys, length - 1, 1))
            out_ref[s, 1, h] = jnp.where(last, v_new, pltpu.roll(vals, length - 1, 1))
        return carry

    lax.fori_loop(0, sb, sample, 0)


SAMPLE_CACHE_BLOCK_BYTES = 4 * 1024 * 1024


def _sample_cache(q_col, new_col, cache_t, g):
    n, _, _, _, length = cache_t.shape
    sb = max(1, SAMPLE_CACHE_BLOCK_BYTES // (2 * H_G * HEAD_DIM * length * 4))
    col = pl.BlockSpec((sb, H_G, HEAD_DIM, 1), lambda i: (i, 0, 0, 0))
    big = pl.BlockSpec((sb, 2, H_G, HEAD_DIM, length), lambda i: (i, 0, 0, 0, 0))
    return pl.pallas_call(
        functools.partial(_sample_cache_kernel, dil=GROUPS[g][1]),
        grid=(n // sb,),
        in_specs=[pl.BlockSpec((sb, None, H_G, HEAD_DIM, 1), lambda i, g=g: (i, g, 0, 0, 0)),
                  pl.BlockSpec((sb, 2, H_G, HEAD_DIM, 1), lambda i: (i, 0, 0, 0, 0)), big],
        out_specs=[big, col, col],
        out_shape=[jax.ShapeDtypeStruct(cache_t.shape, F32),
                   jax.ShapeDtypeStruct((n, H_G, HEAD_DIM, 1), F32),
                   jax.ShapeDtypeStruct((n, H_G, HEAD_DIM, 1), F32)],
        compiler_params=_params(1),
        name="sample_cache_" + "abc"[g],
    )(q_col, new_col, cache_t)


def _rope_tables(pos):
    half = ROT_DIM // 2
    inv_freq = ROPE_THETA ** (-jnp.arange(half, dtype=F32) / half)
    ang = pos.astype(F32)[:, None] * inv_freq[None, :]
    cos, sin = jnp.cos(ang), jnp.sin(ang)
    n = pos.shape[0]
    zeros = lambda w: jnp.zeros((n, w), F32)
    cos_h = jnp.concatenate([cos, cos, jnp.ones((n, HEAD_DIM - ROT_DIM), F32)], axis=-1)
    sa_h = jnp.concatenate([-sin, zeros(HEAD_DIM - half)], axis=-1)
    sb_h = jnp.concatenate([zeros(half), sin, zeros(HEAD_DIM - ROT_DIM)], axis=-1)
    return tuple(jnp.tile(t, (1, H_G)) for t in (cos_h, sa_h, sb_h))


PROMPT_TM = 512


def kernel(x_prompt, x_sample, state_conv, cache_win_a, cache_win_b, cache_win_c, p_prompt, p_sample,
           w_in, g_mix, w_dw, b_dw, ln_g, ln_b, w_conv_out, w_attn_out, w_o, g_ffn, w_ffn_in, w_ffn_out,
           g_ple, w_ple_gate, w_ple_proj, g_final):
    assert w_in.shape[0] == 1, "single-layer step"
    w_in_b = w_in[0].astype(BF16)
    wco = w_conv_out[0].astype(BF16)
    wao = w_attn_out[0].astype(BF16)
    wo = w_o[0].astype(BF16)
    wfi = w_ffn_in[0].astype(BF16)
    wfo = w_ffn_out[0].astype(BF16)
    wpg = w_ple_gate[0].astype(BF16)
    wpp = w_ple_proj[0].astype(BF16)
    gfin = g_final.reshape(1, D_MODEL)
    n_prompt = BATCH * SEQ

    xp = x_prompt.reshape(n_prompt, D_MODEL)
    tabs_p = _rope_tables(jnp.arange(SEQ, dtype=jnp.int32))
    u, q, kva, kvb, kvc, gates = _in_proj(xp, g_mix, w_in_b, *tabs_p, PROMPT_TM)
    per_batch = lambda t: t.reshape(BATCH, SEQ, t.shape[-1])
    o = _attention(per_batch(q), per_batch(kva), per_batch(kvb), per_batch(kvc))
    x1 = _prompt_mix(per_batch(u), x_prompt, o, per_batch(gates), w_dw[0], b_dw, ln_g, ln_b, wco, wao, wo)
    y_prompt = _ffn(x1.reshape(n_prompt, D_MODEL), p_prompt.reshape(n_prompt, PLE_DIM), g_ffn, wfi, wfo,
                    g_ple, wpg, wpp, gfin, PROMPT_TM).reshape(BATCH, SEQ, D_MODEL)
    ctx = CONV_WIDTH - 1
    new_conv_prompt = per_batch(u)[None, :, SEQ - ctx:, :]
    tails = []
    for kv, (win, _) in zip((kva, kvb, kvc), GROUPS):
        length = min(win, PAST_LEN)
        tails.append(per_batch(kv)[:, SEQ - length:, :].reshape(1, BATCH, length, 2, H_G, HEAD_DIM))

    n_s = DEC_BATCH
    xs = x_sample.reshape(n_s, D_MODEL)
    tabs_s = tuple(jnp.broadcast_to(t, (n_s, GROUP_WIDTH))
                   for t in _rope_tables(jnp.full((1,), PAST_LEN, jnp.int32)))
    u_s, q_s, kva_s, kvb_s, kvc_s, gates_s = _in_proj(xs, g_mix, w_in_b, *tabs_s, n_s)
    c_s, new_conv_t = _sample_conv(jnp.transpose(state_conv[0], (1, 0, 2)), u_s, w_dw[0], b_dw)
    q_col = q_s.reshape(n_s, N_GROUPS, H_G, HEAD_DIM, 1)
    new_wins, og, lg = [], [], []
    for g, (kv_new, cache) in enumerate(zip((kva_s, kvb_s, kvc_s), (cache_win_a, cache_win_b, cache_win_c))):
        cache_t = jnp.transpose(cache[0], (0, 2, 3, 4, 1))
        new_t, o_g, lse_g = _sample_cache(q_col, kv_new.reshape(n_s, 2, H_G, HEAD_DIM, 1), cache_t, g)
        new_wins.append(jnp.transpose(new_t, (0, 4, 1, 2, 3))[None])
        og.append(o_g.reshape(n_s, GROUP_WIDTH))
        lg.append(lse_g.reshape(n_s, GROUP_WIDTH))
    x1_s = _sample_mix(c_s, xs, jnp.stack(og), jnp.stack(lg), gates_s, ln_g, ln_b, wco, wao, wo)
    y_sample = _ffn(x1_s, p_sample.reshape(n_s, PLE_DIM), g_ffn, wfi, wfo, g_ple, wpg, wpp, gfin,
                    n_s).reshape(n_s, 1, D_MODEL)
    new_conv_sample = jnp.transpose(new_conv_t, (1, 0, 2))[None]

    return (y_prompt, y_sample, new_conv_prompt, tails[0], tails[1], tails[2],
            new_conv_sample, new_wins[0], new_wins[1], new_wins[2])
```

```python
import functools

import jax
import jax.numpy as jnp
from jax import lax
from jax.experimental import pallas as pl
from jax.experimental.pallas import tpu as pltpu

D_MODEL = 1024
BATCH = 8
SEQ = 2048
DEC_BATCH = 128
PAST_LEN = 8192
HEAD_DIM = 64
GROUPS = ((128, 1), (512, 4), (2048, 16))
H_G = 4
N_GROUPS = len(GROUPS)
GROUP_WIDTH = H_G * HEAD_DIM
ATTN_WIDTH = N_GROUPS * GROUP_WIDTH
ROT_DIM = HEAD_DIM // 4
ROPE_THETA = 500000.0
C_CONV = D_MODEL
CONV_WIDTH = 31
D_FF = 2816
PLE_DIM = 256
NORM_EPS = 1e-6
NK = 128
Q_OFF = 2 * C_CONV
K_OFF = Q_OFF + ATTN_WIDTH
V_OFF = K_OFF + ATTN_WIDTH
GATE_OFF = V_OFF + ATTN_WIDTH
IN_COLS = GATE_OFF + 2 * D_MODEL

F32 = jnp.float32
BF16 = jnp.bfloat16
LANES = 128
SUBLANES = 8
N_SLABS = C_CONV // LANES
MASKED = -1e30
VMEM_LIMIT = 52 * 1024 * 1024


def _const_spec(shape):
    return pl.BlockSpec(shape, lambda *_: (0,) * len(shape), pipeline_mode=pl.Buffered(1))


def _params(n_axes):
    return pltpu.CompilerParams(dimension_semantics=("arbitrary",) * n_axes,
                                vmem_limit_bytes=VMEM_LIMIT)


def _rms(x, g):
    return x * lax.rsqrt(jnp.mean(x * x, axis=-1, keepdims=True) + NORM_EPS) * g


PROMPT_TM = 512
TILES_PER_SEQ = SEQ // PROMPT_TM


def _project(x_ref, g_ref, w_ref, cos_ref, sa_ref, sb_ref, u_ref, gate_ref):
    hb = _rms(x_ref[...], g_ref[...]).astype(BF16)
    ch = GROUP_WIDTH

    def mm(c0):
        return jnp.dot(hb, w_ref[:, c0:c0 + ch], preferred_element_type=F32)

    for c in range(0, C_CONV, ch):
        u = mm(c) * jax.nn.sigmoid(mm(C_CONV + c))
        for j in range(ch // LANES):
            u_ref[c // LANES + j] = u[:, j * LANES:(j + 1) * LANES]

    for c in range(0, 2 * D_MODEL, ch):
        gate_ref[:, c:c + ch] = jax.nn.sigmoid(mm(GATE_OFF + c)).astype(BF16)

    cos = cos_ref[...]
    sa = sa_ref[...]
    sb = sb_ref[...]

    def rope(z):
        return z * cos + pltpu.roll(z, ch - ROT_DIM // 2, 1) * sa + pltpu.roll(z, ROT_DIM // 2, 1) * sb

    def qkv(g):
        c = g * ch
        return rope(mm(Q_OFF + c)) * (HEAD_DIM ** -0.5), rope(mm(K_OFF + c)), mm(V_OFF + c)

    return qkv


def _in_proj_prompt_kernel(x_ref, g_ref, w_ref, cos_ref, sa_ref, sb_ref,
                           u_ref, q_ref, kva_ref, kvb_ref, kvc_ref, gate_ref, ta_ref, tb_ref, tc_ref):
    qkv = _project(x_ref, g_ref, w_ref, cos_ref, sa_ref, sb_ref, u_ref, gate_ref)
    ch = GROUP_WIDTH
    last_tile = pl.program_id(0) % TILES_PER_SEQ == TILES_PER_SEQ - 1
    kv_refs = (kva_ref, kvb_ref, kvc_ref)
    for g in range(N_GROUPS):
        q, k, v = qkv(g)
        q_ref[:, g * ch:(g + 1) * ch] = q
        kv_refs[g][:, 0:ch] = k
        kv_refs[g][:, ch:2 * ch] = v
        if g == 2:
            tc_ref[0:ch, :] = k.T
            tc_ref[ch:2 * ch, :] = v.T
        elif g == 1:
            @pl.when(last_tile)
            def _(k=k, v=v):
                tb_ref[0:ch, :] = k.T
                tb_ref[ch:2 * ch, :] = v.T
        else:
            @pl.when(last_tile)
            def _(k=k, v=v):
                n = ta_ref.shape[1]
                ta_ref[0:ch, :] = k[PROMPT_TM - n:, :].T
                ta_ref[ch:2 * ch, :] = v[PROMPT_TM - n:, :].T


def _in_proj_prompt(x2d, g, w, cos, sa, sb):
    tm = PROMPT_TM
    rows = x2d.shape[0]
    assert GROUPS[1][0] == tm and GROUPS[2][0] == SEQ and GROUPS[0][0] <= tm
    row = lambda n: pl.BlockSpec((tm, n), lambda i: (i, 0))
    tab = pl.BlockSpec((tm, GROUP_WIDTH), lambda i: (i % TILES_PER_SEQ, 0))
    kv_w = 2 * GROUP_WIDTH
    tail = lambda n: pl.BlockSpec((None, kv_w, n), lambda i: (i // TILES_PER_SEQ, 0, 0))
    return pl.pallas_call(
        _in_proj_prompt_kernel,
        grid=(rows // tm,),
        in_specs=[row(D_MODEL), _const_spec((1, D_MODEL)), _const_spec((D_MODEL, IN_COLS)), tab, tab, tab],
        out_specs=[pl.BlockSpec((N_SLABS, tm, LANES), lambda i: (0, i, 0)),
                   row(ATTN_WIDTH), row(kv_w), row(kv_w), row(kv_w), row(2 * D_MODEL),
                   tail(GROUPS[0][0]), tail(GROUPS[1][0]),
                   pl.BlockSpec((None, kv_w, tm), lambda i: (i // TILES_PER_SEQ, 0, i % TILES_PER_SEQ))],
        out_shape=[jax.ShapeDtypeStruct((N_SLABS, rows, LANES), F32),
                   jax.ShapeDtypeStruct((rows, ATTN_WIDTH), F32),
                   jax.ShapeDtypeStruct((rows, kv_w), F32),
                   jax.ShapeDtypeStruct((rows, kv_w), F32),
                   jax.ShapeDtypeStruct((rows, kv_w), F32),
                   jax.ShapeDtypeStruct((rows, 2 * D_MODEL), BF16),
                   jax.ShapeDtypeStruct((BATCH, kv_w, GROUPS[0][0]), F32),
                   jax.ShapeDtypeStruct((BATCH, kv_w, GROUPS[1][0]), F32),
                   jax.ShapeDtypeStruct((BATCH, kv_w, SEQ), F32)],
        compiler_params=_params(1),
        name="in_proj_prompt",
    )(x2d, g, w, cos, sa, sb)


def _in_proj_sample_kernel(x_ref, g_ref, w_ref, cos_ref, sa_ref, sb_ref,
                           u_ref, qt_ref, ta_ref, tb_ref, tc_ref, gate_ref):
    qkv = _project(x_ref, g_ref, w_ref, cos_ref, sa_ref, sb_ref, u_ref, gate_ref)
    ch = GROUP_WIDTH
    for g, t_ref in enumerate((ta_ref, tb_ref, tc_ref)):
        q, k, v = qkv(g)
        qt_ref[g * ch:(g + 1) * ch, :] = q.T
        t_ref[0:ch, :] = k.T
        t_ref[ch:2 * ch, :] = v.T


def _in_proj_sample(x2d, g, w, cos, sa, sb):
    n = x2d.shape[0]
    full = lambda r, c: pl.BlockSpec((r, c), lambda i: (0, 0))
    kv_w = 2 * GROUP_WIDTH
    return pl.pallas_call(
        _in_proj_sample_kernel,
        grid=(1,),
        in_specs=[full(n, D_MODEL), _const_spec((1, D_MODEL)), _const_spec((D_MODEL, IN_COLS)),
                  full(n, GROUP_WIDTH), full(n, GROUP_WIDTH), full(n, GROUP_WIDTH)],
        out_specs=[pl.BlockSpec((N_SLABS, n, LANES), lambda i: (0, 0, 0)),
                   full(ATTN_WIDTH, n), full(kv_w, n), full(kv_w, n), full(kv_w, n), full(n, 2 * D_MODEL)],
        out_shape=[jax.ShapeDtypeStruct((N_SLABS, n, LANES), F32),
                   jax.ShapeDtypeStruct((ATTN_WIDTH, n), F32),
                   jax.ShapeDtypeStruct((kv_w, n), F32),
                   jax.ShapeDtypeStruct((kv_w, n), F32),
                   jax.ShapeDtypeStruct((kv_w, n), F32),
                   jax.ShapeDtypeStruct((n, 2 * D_MODEL), BF16)],
        compiler_params=_params(1),
        name="in_proj_sample",
    )(x2d, g, w, cos, sa, sb)


ATTN_UNROLL = 4


def _combine_groups(lses, outs):
    mx = jnp.maximum(jnp.maximum(lses[0], lses[1]), lses[2])
    es = [jnp.exp(l - mx) for l in lses]
    num = es[0] * outs[0] + es[1] * outs[1] + es[2] * outs[2]
    return num / (es[0] + es[1] + es[2])


def _largest_divisor(n, cap):
    return max(d for d in range(1, cap + 1) if n % d == 0)


def _attn_kernel(qa_ref, qb_ref, qc_ref, ka_ref, va_ref, kb_ref, vb_ref, kc_ref, vc_ref,
                 o_ref, og_ref, lg_ref):
    lane = lax.broadcasted_iota(jnp.int32, (NK, LANES), 1)
    qi = lax.broadcasted_iota(jnp.int32, (NK, LANES), 0)
    first_head = lane < HEAD_DIM
    own_mask = lane <= qi
    prev_mask = lane >= qi
    nt = (((1,), (1,)), ((), ()))

    def block(g, q_ref, k_ref, v_ref, rows, prev_rows):
        qb = q_ref[rows, :]
        ko = k_ref[rows, :].astype(BF16)
        vo = v_ref[rows, :].astype(BF16)
        if prev_rows is not None:
            kp = k_ref[prev_rows, :].astype(BF16)
            vp = v_ref[prev_rows, :].astype(BF16)
            vals = jnp.concatenate([vo, vp], axis=0)
        else:
            vals = vo
        ones = jnp.ones((vals.shape[0], LANES), BF16)
        outs, lses = [], []
        for head_mask in (first_head, jnp.logical_not(first_head)):
            qh = jnp.where(head_mask, qb, 0.0).astype(BF16)
            so = jnp.where(own_mask, lax.dot_general(qh, ko, nt, preferred_element_type=F32), MASKED)
            if prev_rows is not None:
                sp = jnp.where(prev_mask, lax.dot_general(qh, kp, nt, preferred_element_type=F32), MASKED)
                m = jnp.max(jnp.maximum(so, sp), axis=-1, keepdims=True)
                p = jnp.concatenate([jnp.exp(so - m), jnp.exp(sp - m)], axis=1).astype(BF16)
            else:
                m = jnp.max(so, axis=-1, keepdims=True)
                p = jnp.exp(so - m).astype(BF16)
            l = jnp.dot(p, ones, preferred_element_type=F32)
            outs.append(jnp.dot(p, vals, preferred_element_type=F32) / l)
            lses.append(m + jnp.log(l))
        og_ref[g, rows, :] = jnp.where(first_head, outs[0], outs[1])
        lg_ref[g, rows, :] = jnp.where(first_head, lses[0], lses[1])

    for g, (q_ref, k_ref, v_ref) in enumerate(((qa_ref, ka_ref, va_ref), (qb_ref, kb_ref, vb_ref),
                                               (qc_ref, kc_ref, vc_ref))):
        dil = GROUPS[g][1]
        span = NK * dil
        n_later = dil * (SEQ // span - 1)

        def rows_at(start, dil=dil):
            return pl.ds(start, NK) if dil == 1 else pl.ds(start, NK, stride=dil)

        def first(r, carry, g=g, q_ref=q_ref, k_ref=k_ref, v_ref=v_ref, rows_at=rows_at):
            block(g, q_ref, k_ref, v_ref, rows_at(r), None)
            return carry

        def later(i, carry, g=g, q_ref=q_ref, k_ref=k_ref, v_ref=v_ref, rows_at=rows_at, dil=dil, span=span):
            start = i % dil + (1 + i // dil) * span
            block(g, q_ref, k_ref, v_ref, rows_at(start), rows_at(start - span))
            return carry

        lax.fori_loop(0, dil, first, 0, unroll=_largest_divisor(dil, ATTN_UNROLL))
        if n_later:
            lax.fori_loop(0, n_later, later, 0, unroll=_largest_divisor(n_later, ATTN_UNROLL))

    tc = 256
    for t in range(0, SEQ, tc):
        rows = slice(t, t + tc)
        o_ref[rows, :] = _combine_groups([lg_ref[g, rows, :] for g in range(N_GROUPS)],
                                         [og_ref[g, rows, :] for g in range(N_GROUPS)]).astype(o_ref.dtype)


def _attention(q3, kva3, kvb3, kvc3):
    def col(c):
        return pl.BlockSpec((None, SEQ, LANES), lambda b, hp, c=c: (b, 0, c + hp))
    n_pairs = GROUP_WIDTH // LANES
    return pl.pallas_call(
        _attn_kernel,
        grid=(BATCH, n_pairs),
        in_specs=[col(0), col(n_pairs), col(2 * n_pairs),
                  col(0), col(n_pairs), col(0), col(n_pairs), col(0), col(n_pairs)],
        out_specs=pl.BlockSpec((None, SEQ, LANES), lambda b, hp: (b, 0, hp)),
        out_shape=jax.ShapeDtypeStruct((BATCH, SEQ, GROUP_WIDTH), BF16),
        scratch_shapes=[pltpu.VMEM((N_GROUPS, SEQ, LANES), F32), pltpu.VMEM((N_GROUPS, SEQ, LANES), F32)],
        compiler_params=_params(2),
        name="prompt_attention",
    )(q3, q3, q3, kva3, kva3, kvb3, kvb3, kvc3, kvc3)


def _merge(c, x, ob, gate_ref, lng_ref, lnb_ref, wco_ref, wao_ref, wo_ref):
    mu = jnp.mean(c, axis=-1, keepdims=True)
    d = c - mu
    var = jnp.mean(d * d, axis=-1, keepdims=True)
    y = d * lax.rsqrt(var + NORM_EPS) * lng_ref[...] + lnb_ref[...]
    a_out = jnp.dot(jax.nn.silu(y).astype(BF16), wco_ref[...], preferred_element_type=F32)
    b_out = jnp.dot(ob, wao_ref[...], preferred_element_type=F32)
    merged = (gate_ref[:, 0:D_MODEL].astype(F32) * a_out
              + gate_ref[:, D_MODEL:2 * D_MODEL].astype(F32) * b_out)
    return x + jnp.dot(merged.astype(BF16), wo_ref[...], preferred_element_type=F32)


CONV_TS = 256
CONV_HALO = 32
CONV_PITCH = 4
CONV_GROUP = CONV_PITCH * SUBLANES
CONV_UNROLL = 2


def _prompt_mix_kernel(u_ref, x_ref, o_ref, gate_ref, wdw_ref, bdw_ref, lng_ref, lnb_ref,
                       wco_ref, wao_ref, wo_ref, x1_ref, win_ref, c_ref):
    j = pl.program_id(1)
    t0 = pl.multiple_of(j * CONV_TS, CONV_TS)

    @pl.when(j == 0)
    def _():
        win_ref[:, 0:CONV_HALO, :] = jnp.zeros((N_SLABS, CONV_HALO, LANES), F32)

    @pl.when(j > 0)
    def _():
        win_ref[:, 0:CONV_HALO, :] = u_ref[:, pl.ds(t0 - CONV_HALO, CONV_HALO), :]

    win_ref[:, CONV_HALO:CONV_HALO + CONV_TS, :] = u_ref[:, pl.ds(t0, CONV_TS), :]
    first = CONV_HALO - (CONV_WIDTH - 1)

    def slab(lc, carry):
        l0 = pl.multiple_of(lc * LANES, LANES)
        bias = jnp.broadcast_to(bdw_ref[:, pl.ds(l0, LANES)], (SUBLANES, LANES))

        def groups(gi, carry2):
            base = pl.multiple_of(gi * (CONV_UNROLL * CONV_GROUP), CONV_UNROLL * CONV_GROUP)
            n_acc = CONV_UNROLL * CONV_PITCH
            offs = [(i // CONV_PITCH) * CONV_GROUP + i % CONV_PITCH for i in range(n_acc)]
            accs = [bias] * n_acc
            for k in range(CONV_WIDTH):
                wk = wdw_ref[k:k + 1, pl.ds(l0, LANES)]
                for i in range(n_acc):
                    tap = win_ref[lc, pl.ds(base + first + k + offs[i], SUBLANES, stride=CONV_PITCH), :]
                    accs[i] = accs[i] + tap * wk
            for i in range(n_acc):
                c_ref[lc, pl.ds(base + offs[i], SUBLANES, stride=CONV_PITCH), :] = accs[i]
            return carry2

        lax.fori_loop(0, CONV_TS // (CONV_UNROLL * CONV_GROUP), groups, 0)
        return carry

    lax.fori_loop(0, N_SLABS, slab, 0)
    c = jnp.concatenate([c_ref[lc] for lc in range(N_SLABS)], axis=-1)
    x1_ref[...] = _merge(c, x_ref[...], o_ref[...], gate_ref, lng_ref, lnb_ref, wco_ref, wao_ref, wo_ref)


def _prompt_mix(u4, x3, o3, gate3, wdw, bdw, lng, lnb, wco, wao, wo):
    tile = lambda n: pl.BlockSpec((None, CONV_TS, n), lambda b, j: (b, j, 0))
    return pl.pallas_call(
        _prompt_mix_kernel,
        grid=(BATCH, SEQ // CONV_TS),
        in_specs=[pl.BlockSpec((N_SLABS, None, SEQ, LANES), lambda b, j: (0, b, 0, 0)),
                  tile(D_MODEL), tile(GROUP_WIDTH), tile(2 * D_MODEL),
                  _const_spec((CONV_WIDTH, C_CONV)), _const_spec((1, C_CONV)),
                  _const_spec((1, C_CONV)), _const_spec((1, C_CONV)),
                  _const_spec((C_CONV, D_MODEL)), _const_spec((GROUP_WIDTH, D_MODEL)),
                  _const_spec((D_MODEL, D_MODEL))],
        out_specs=tile(D_MODEL),
        out_shape=jax.ShapeDtypeStruct((BATCH, SEQ, D_MODEL), F32),
        scratch_shapes=[pltpu.VMEM((N_SLABS, CONV_HALO + CONV_TS, LANES), F32),
                        pltpu.VMEM((N_SLABS, CONV_TS, LANES), F32)],
        compiler_params=_params(2),
        name="prompt_mix",
    )(u4, x3, o3, gate3, wdw, bdw, lng, lnb, wco, wao, wo)


def _sample_mix_kernel(c_ref, x_ref, oa_ref, ob_ref, oc_ref, la_ref, lb_ref, lc_ref, gate_ref,
                       lng_ref, lnb_ref, wco_ref, wao_ref, wo_ref, x1_ref):
    ob = _combine_groups([r[...].T for r in (la_ref, lb_ref, lc_ref)],
                         [r[...].T for r in (oa_ref, ob_ref, oc_ref)]).astype(BF16)
    x1_ref[...] = _merge(c_ref[...], x_ref[...], ob, gate_ref, lng_ref, lnb_ref, wco_ref, wao_ref, wo_ref)


def _sample_mix(c, x, ogs, lgs, gate, lng, lnb, wco, wao, wo):
    rows = x.shape[0]
    full = lambda n: _const_spec((rows, n))
    per_group = _const_spec((GROUP_WIDTH, rows))
    return pl.pallas_call(
        _sample_mix_kernel,
        grid=(1,),
        in_specs=[full(C_CONV), full(D_MODEL)] + [per_group] * (2 * N_GROUPS) + [full(2 * D_MODEL),
                  _const_spec((1, C_CONV)), _const_spec((1, C_CONV)),
                  _const_spec((C_CONV, D_MODEL)), _const_spec((GROUP_WIDTH, D_MODEL)),
                  _const_spec((D_MODEL, D_MODEL))],
        out_specs=pl.BlockSpec((rows, D_MODEL), lambda i: (0, 0)),
        out_shape=jax.ShapeDtypeStruct((rows, D_MODEL), F32),
        compiler_params=_params(1),
        name="sample_mix",
    )(c, x, *ogs, *lgs, gate, lng, lnb, wco, wao, wo)


FF_CHUNK = 256


def _ffn_kernel(x_ref, p_ref, gf_ref, wfi_ref, wfo_ref, gp_ref, wpg_ref, wpp_ref, gfin_ref, y_ref, acc_ref):
    x1 = x_ref[...]
    hb = _rms(x1, gf_ref[...]).astype(BF16)
    acc_ref[...] = jnp.zeros_like(acc_ref)
    for c in range(0, D_FF, FF_CHUNK):
        gch = jnp.dot(hb, wfi_ref[:, c:c + FF_CHUNK], preferred_element_type=F32)
        uch = jnp.dot(hb, wfi_ref[:, D_FF + c:D_FF + c + FF_CHUNK], preferred_element_type=F32)
        act = (jax.nn.silu(gch) * uch).astype(BF16)
        acc_ref[...] += jnp.dot(act, wfo_ref[c:c + FF_CHUNK, :], preferred_element_type=F32)
    x2 = x1 + acc_ref[...]
    hp = _rms(x2, gp_ref[...]).astype(BF16)
    gate = jax.nn.sigmoid(jnp.dot(hp, wpg_ref[...], preferred_element_type=F32))
    pe = jnp.dot(p_ref[...].astype(BF16), wpp_ref[...], preferred_element_type=F32)
    x3 = x2 + gate * pe
    y_ref[...] = _rms(x3, gfin_ref[...])


def _ffn(x1, p, gf, wfi, wfo, gp, wpg, wpp, gfin, tm):
    rows = x1.shape[0]
    row = lambda n: pl.BlockSpec((tm, n), lambda i: (i, 0))
    return pl.pallas_call(
        _ffn_kernel,
        grid=(rows // tm,),
        in_specs=[row(D_MODEL), row(PLE_DIM), _const_spec((1, D_MODEL)),
                  _const_spec((D_MODEL, 2 * D_FF)), _const_spec((D_FF, D_MODEL)),
                  _const_spec((1, D_MODEL)), _const_spec((D_MODEL, D_MODEL)),
                  _const_spec((PLE_DIM, D_MODEL)), _const_spec((1, D_MODEL))],
        out_specs=row(D_MODEL),
        out_shape=jax.ShapeDtypeStruct((rows, D_MODEL), F32),
        scratch_shapes=[pltpu.VMEM((tm, D_MODEL), F32)],
        compiler_params=_params(1),
        name="ffn_ple",
    )(x1, p, gf, wfi, wfo, gp, wpg, wpp, gfin)


SAMPLE_CONV_BLOCK = 32


def _sample_conv_kernel(state_ref, u_ref, wdw_ref, bdw_ref, c_ref, new_ref):
    ctx = CONV_WIDTH - 1
    for lc in range(N_SLABS):
        cols = slice(lc * LANES, (lc + 1) * LANES)
        u = u_ref[lc]
        acc = u * wdw_ref[ctx:ctx + 1, cols] + bdw_ref[:, cols]
        for k in range(ctx):
            acc = acc + state_ref[k, :, cols] * wdw_ref[k:k + 1, cols]
        c_ref[:, cols] = acc
        new_ref[ctx - 1, :, cols] = u
    for k in range(ctx - 1):
        new_ref[k] = state_ref[k + 1]


def _sample_conv(state_t, u_slabs, wdw, bdw):
    ctx, n, _ = state_t.shape
    sb = SAMPLE_CONV_BLOCK
    return pl.pallas_call(
        _sample_conv_kernel,
        grid=(n // sb,),
        in_specs=[pl.BlockSpec((ctx, sb, C_CONV), lambda i: (0, i, 0)),
                  pl.BlockSpec((N_SLABS, sb, LANES), lambda i: (0, i, 0)),
                  _const_spec((CONV_WIDTH, C_CONV)), _const_spec((1, C_CONV))],
        out_specs=[pl.BlockSpec((sb, C_CONV), lambda i: (i, 0)),
                   pl.BlockSpec((ctx, sb, C_CONV), lambda i: (0, i, 0))],
        out_shape=[jax.ShapeDtypeStruct((n, C_CONV), F32),
                   jax.ShapeDtypeStruct((ctx, n, C_CONV), F32)],
        compiler_params=_params(1),
        name="sample_conv",
    )(state_t, u_slabs, wdw, bdw)


def _sample_cache_kernel(q_ref, new_ref, cache_ref, out_ref, o_ref, lse_ref, *, dil):
    sb, _, _, _, length = cache_ref.shape
    step = pl.program_id(0)

    @pl.when(step == 0)
    def _():
        o_ref[...] = jnp.zeros_like(o_ref)
        lse_ref[...] = jnp.zeros_like(lse_ref)

    pos = lax.broadcasted_iota(jnp.int32, (1, length), 1)
    used = (pos & (dil - 1)) == 0
    lane = lax.broadcasted_iota(jnp.int32, (HEAD_DIM, LANES), 1)
    last = lane == LANES - 1

    def sample(s, carry):
        mine = lane == step * sb + s

        def pick(ref, r0):
            return jnp.sum(jnp.where(mine, ref[r0:r0 + HEAD_DIM, :], 0.0), axis=1, keepdims=True)

        for h in range(H_G):
            rows = slice(h * HEAD_DIM, (h + 1) * HEAD_DIM)
            q = pick(q_ref, h * HEAD_DIM)
            k_new = pick(new_ref, h * HEAD_DIM)
            v_new = pick(new_ref, GROUP_WIDTH + h * HEAD_DIM)
            keys = cache_ref[s, 0, h]
            vals = cache_ref[s, 1, h]
            sc = jnp.where(used, jnp.sum(keys * q, axis=0, keepdims=True), MASKED)
            s_new = jnp.sum(k_new * q, axis=0, keepdims=True)
            m = jnp.maximum(jnp.max(sc, axis=1, keepdims=True), s_new)
            p = jnp.exp(sc - m)
            p_new = jnp.exp(s_new - m)
            l = jnp.sum(p, axis=1, keepdims=True) + p_new
            o = (jnp.sum(vals * p, axis=1, keepdims=True) + v_new * p_new) / l
            o_ref[rows, :] = jnp.where(mine, o, o_ref[rows, :])
            lse_ref[rows, :] = jnp.where(mine, m + jnp.log(l), lse_ref[rows, :])
            for kv, (old, new) in enumerate(((keys, k_new), (vals, v_new))):
                rolled = pltpu.roll(old, length - 1, 1)
                out_ref[s, kv, h] = rolled
                out_ref[s, kv, h, :, length - LANES:] = jnp.where(last, new, rolled[:, length - LANES:])
        return carry

    lax.fori_loop(0, sb, sample, 0)


SAMPLE_CACHE_BLOCK_BYTES = 4 * 1024 * 1024


def _sample_cache(q_t, new_t, cache_t, g):
    n, _, _, _, length = cache_t.shape
    sb = max(1, SAMPLE_CACHE_BLOCK_BYTES // (2 * H_G * HEAD_DIM * length * 4))
    big = pl.BlockSpec((sb, 2, H_G, HEAD_DIM, length), lambda i: (i, 0, 0, 0, 0))
    res = pl.BlockSpec((GROUP_WIDTH, n), lambda i: (0, 0))
    return pl.pallas_call(
        functools.partial(_sample_cache_kernel, dil=GROUPS[g][1]),
        grid=(n // sb,),
        in_specs=[pl.BlockSpec((GROUP_WIDTH, n), lambda i, g=g: (g, 0)),
                  pl.BlockSpec((2 * GROUP_WIDTH, n), lambda i: (0, 0)), big],
        out_specs=[big, res, res],
        out_shape=[jax.ShapeDtypeStruct(cache_t.shape, F32),
                   jax.ShapeDtypeStruct((GROUP_WIDTH, n), F32),
                   jax.ShapeDtypeStruct((GROUP_WIDTH, n), F32)],
        compiler_params=_params(1),
        name="sample_cache_" + "abc"[g],
    )(q_t, new_t, cache_t)


def _rope_tables(pos):
    half = ROT_DIM // 2
    inv_freq = ROPE_THETA ** (-jnp.arange(half, dtype=F32) / half)
    ang = pos.astype(F32)[:, None] * inv_freq[None, :]
    cos, sin = jnp.cos(ang), jnp.sin(ang)
    n = pos.shape[0]
    zeros = lambda w: jnp.zeros((n, w), F32)
    cos_h = jnp.concatenate([cos, cos, jnp.ones((n, HEAD_DIM - ROT_DIM), F32)], axis=-1)
    sa_h = jnp.concatenate([-sin, zeros(HEAD_DIM - half)], axis=-1)
    sb_h = jnp.concatenate([zeros(half), sin, zeros(HEAD_DIM - ROT_DIM)], axis=-1)
    return tuple(jnp.tile(t, (1, H_G)) for t in (cos_h, sa_h, sb_h))


def _position_major(t):
    b, _, length = t.shape
    return jnp.transpose(t.reshape(b, 2, H_G, HEAD_DIM, length), (0, 4, 1, 2, 3))[None]


def kernel(x_prompt, x_sample, state_conv, cache_win_a, cache_win_b, cache_win_c, p_prompt, p_sample,
           w_in, g_mix, w_dw, b_dw, ln_g, ln_b, w_conv_out, w_attn_out, w_o, g_ffn, w_ffn_in, w_ffn_out,
           g_ple, w_ple_gate, w_ple_proj, g_final):
    assert w_in.shape[0] == 1, "single-layer step"
    w_in_b = w_in[0].astype(BF16)
    wco = w_conv_out[0].astype(BF16)
    wao = w_attn_out[0].astype(BF16)
    wo = w_o[0].astype(BF16)
    wfi = w_ffn_in[0].astype(BF16)
    wfo = w_ffn_out[0].astype(BF16)
    wpg = w_ple_gate[0].astype(BF16)
    wpp = w_ple_proj[0].astype(BF16)
    gfin = g_final.reshape(1, D_MODEL)
    n_prompt = BATCH * SEQ
    ctx = CONV_WIDTH - 1

    xp = x_prompt.reshape(n_prompt, D_MODEL)
    tabs_p = _rope_tables(jnp.arange(SEQ, dtype=jnp.int32))
    u, q, kva, kvb, kvc, gates, tail_a, tail_b, tail_c = _in_proj_prompt(xp, g_mix, w_in_b, *tabs_p)
    per_batch = lambda t: t.reshape(BATCH, SEQ, t.shape[-1])
    o = _attention(per_batch(q), per_batch(kva), per_batch(kvb), per_batch(kvc))
    u4 = u.reshape(N_SLABS, BATCH, SEQ, LANES)
    x1 = _prompt_mix(u4, x_prompt, o, per_batch(gates), w_dw[0], b_dw, ln_g, ln_b, wco, wao, wo)
    y_prompt = _ffn(x1.reshape(n_prompt, D_MODEL), p_prompt.reshape(n_prompt, PLE_DIM), g_ffn, wfi, wfo,
                    g_ple, wpg, wpp, gfin, PROMPT_TM).reshape(BATCH, SEQ, D_MODEL)
    new_conv_prompt = jnp.transpose(u4[:, :, SEQ - ctx:, :], (1, 2, 0, 3)).reshape(1, BATCH, ctx, C_CONV)

    n_s = DEC_BATCH
    xs = x_sample.reshape(n_s, D_MODEL)
    tabs_s = tuple(jnp.broadcast_to(t, (n_s, GROUP_WIDTH))
                   for t in _rope_tables(jnp.full((1,), PAST_LEN, jnp.int32)))
    u_s, q_t, new_a, new_b, new_c, gates_s = _in_proj_sample(xs, g_mix, w_in_b, *tabs_s)
    c_s, new_conv_t = _sample_conv(jnp.transpose(state_conv[0], (1, 0, 2)), u_s, w_dw[0], b_dw)
    new_wins, ogs, lgs = [], [], []
    for g, (new_t, cache) in enumerate(zip((new_a, new_b, new_c), (cache_win_a, cache_win_b, cache_win_c))):
        cache_t = jnp.transpose(cache[0], (0, 2, 3, 4, 1))
        shifted, o_g, lse_g = _sample_cache(q_t, new_t, cache_t, g)
        new_wins.append(jnp.transpose(shifted, (0, 4, 1, 2, 3))[None])
        ogs.append(o_g)
        lgs.append(lse_g)
    x1_s = _sample_mix(c_s, xs, ogs, lgs, gates_s, ln_g, ln_b, wco, wao, wo)
    y_sample = _ffn(x1_s, p_sample.reshape(n_s, PLE_DIM), g_ffn, wfi, wfo, g_ple, wpg, wpp, gfin,
                    n_s).reshape(n_s, 1, D_MODEL)
    new_conv_sample = jnp.transpose(new_conv_t, (1, 0, 2))[None]

    return (y_prompt, y_sample, new_conv_prompt, _position_major(tail_a), _position_major(tail_b),
            _position_major(tail_c), new_conv_sample, new_wins[0], new_wins[1], new_wins[2])
```

```python
import functools

import jax
import jax.numpy as jnp
from jax import lax
from jax.experimental import pallas as pl
from jax.experimental.pallas import tpu as pltpu

D_MODEL = 1024
BATCH = 8
SEQ = 2048
DEC_BATCH = 128
PAST_LEN = 8192
HEAD_DIM = 64
GROUPS = ((128, 1), (512, 4), (2048, 16))
H_G = 4
N_GROUPS = len(GROUPS)
GROUP_WIDTH = H_G * HEAD_DIM
ATTN_WIDTH = N_GROUPS * GROUP_WIDTH
ROT_DIM = HEAD_DIM // 4
ROPE_THETA = 500000.0
C_CONV = D_MODEL
CONV_WIDTH = 31
D_FF = 2816
PLE_DIM = 256
NORM_EPS = 1e-6
NK = 128
Q_OFF = 2 * C_CONV
K_OFF = Q_OFF + ATTN_WIDTH
V_OFF = K_OFF + ATTN_WIDTH
GATE_OFF = V_OFF + ATTN_WIDTH
IN_COLS = GATE_OFF + 2 * D_MODEL

F32 = jnp.float32
BF16 = jnp.bfloat16
LANES = 128
SUBLANES = 8
N_SLABS = C_CONV // LANES
MASKED = -1e30
VMEM_LIMIT = 52 * 1024 * 1024


def _const_spec(shape):
    return pl.BlockSpec(shape, lambda *_: (0,) * len(shape), pipeline_mode=pl.Buffered(1))


def _params(n_axes):
    return pltpu.CompilerParams(dimension_semantics=("arbitrary",) * n_axes,
                                vmem_limit_bytes=VMEM_LIMIT)


def _rms(x, g):
    return x * lax.rsqrt(jnp.mean(x * x, axis=-1, keepdims=True) + NORM_EPS) * g


PROMPT_TM = 512
TILES_PER_SEQ = SEQ // PROMPT_TM
CONV_HALO = 32
CONV_PITCH = 4
CONV_GROUP = CONV_PITCH * SUBLANES
CONV_UNROLL = 1


def _projector(x_ref, g_ref, w_ref, cos_ref, sa_ref, sb_ref):
    hb = _rms(x_ref[...], g_ref[...]).astype(BF16)
    ch = GROUP_WIDTH

    def mm(c0):
        return jnp.dot(hb, w_ref[:, c0:c0 + ch], preferred_element_type=F32)

    def glu(c):
        return mm(c) * jax.nn.sigmoid(mm(C_CONV + c))

    def gates(gate_ref, lo=0, hi=2 * D_MODEL // ch):
        for c in range(lo * ch, hi * ch, ch):
            gate_ref[:, c:c + ch] = jax.nn.sigmoid(mm(GATE_OFF + c)).astype(BF16)

    def rope(z):
        return (z * cos_ref[...] + pltpu.roll(z, ch - ROT_DIM // 2, 1) * sa_ref[...]
                + pltpu.roll(z, ROT_DIM // 2, 1) * sb_ref[...])

    def qkv(g):
        c = g * ch
        return rope(mm(Q_OFF + c)) * (HEAD_DIM ** -0.5), rope(mm(K_OFF + c)), mm(V_OFF + c)

    return qkv, glu, gates


def _conv_slab(win_ref, lc, wdw_ref, bdw_ref, c_ref, n_rows):
    first = CONV_HALO - (CONV_WIDTH - 1)
    cols = slice(lc * LANES, (lc + 1) * LANES)
    n_acc = CONV_UNROLL * CONV_PITCH
    offs = [(i // CONV_PITCH) * CONV_GROUP + i % CONV_PITCH for i in range(n_acc)]
    bias = jnp.broadcast_to(bdw_ref[:, cols], (SUBLANES, LANES))
    for base in range(0, n_rows, CONV_UNROLL * CONV_GROUP):
        accs = [bias] * n_acc
        for k in range(CONV_WIDTH):
            wk = wdw_ref[k:k + 1, cols]
            for i in range(n_acc):
                tap = win_ref[lc, pl.ds(base + first + k + offs[i], SUBLANES, stride=CONV_PITCH), :]
                accs[i] = accs[i] + tap * wk
        for i in range(n_acc):
            c_ref[lc, pl.ds(base + offs[i], SUBLANES, stride=CONV_PITCH), :] = accs[i]


def _in_proj_prompt_kernel(x_ref, g_ref, w_ref, cos_ref, sa_ref, sb_ref, wdw_ref, bdw_ref,
                           c_ref, q_ref, kva_ref, kvb_ref, kvc_ref, gate_ref, ta_ref, tb_ref, tc_ref, ut_ref,
                           win_ref):
    tm = PROMPT_TM
    ch = GROUP_WIDTH

    @pl.when(pl.program_id(0) == 0)
    def _():
        win_ref[:, tm:tm + CONV_HALO, :] = jnp.zeros((N_SLABS, CONV_HALO, LANES), F32)

    qkv, glu, gates = _projector(x_ref, g_ref, w_ref, cos_ref, sa_ref, sb_ref)
    first_tile = pl.program_id(0) % TILES_PER_SEQ == 0
    def head_group(g):
        kv_ref, t_ref = ((kva_ref, ta_ref), (kvb_ref, tb_ref), (kvc_ref, tc_ref))[g]
        q, k, v = qkv(g)
        q_ref[:, g * ch:(g + 1) * ch] = q
        kv_ref[:, 0:ch] = k
        kv_ref[:, ch:2 * ch] = v
        n = t_ref.shape[1]
        t_ref[0:ch, :] = k[tm - n:, :].T
        t_ref[ch:2 * ch, :] = v[tm - n:, :].T

    def glu_chunk(i):
        c = i * ch
        u = glu(c)
        ut_ref[:, c:c + ch] = u[tm - CONV_HALO:, :]
        for j in range(ch // LANES):
            lc = c // LANES + j
            win_ref[lc, 0:CONV_HALO, :] = jnp.where(first_tile, 0.0, win_ref[lc, tm:tm + CONV_HALO, :])
            win_ref[lc, CONV_HALO:CONV_HALO + tm, :] = u[:, j * LANES:(j + 1) * LANES]

    n_gate = 2 * D_MODEL // ch
    after_slab = [
        lambda: glu_chunk(1), lambda: head_group(0), lambda: glu_chunk(2), lambda: head_group(1),
        lambda: glu_chunk(3), lambda: head_group(2),
        lambda: gates(gate_ref, 0, n_gate // 2), lambda: gates(gate_ref, n_gate // 2, n_gate)]
    glu_chunk(0)
    for lc in range(N_SLABS):
        _conv_slab(win_ref, lc, wdw_ref, bdw_ref, c_ref, tm)
        after_slab[lc]()


def _in_proj_prompt(x2d, g, w, cos, sa, sb, wdw, bdw):
    tm = PROMPT_TM
    rows = x2d.shape[0]
    assert GROUPS[1][0] == tm and GROUPS[2][0] == SEQ and GROUPS[0][0] <= tm
    row = lambda n: pl.BlockSpec((tm, n), lambda i: (i, 0))
    tab = pl.BlockSpec((tm, GROUP_WIDTH), lambda i: (i % TILES_PER_SEQ, 0))
    kv_w = 2 * GROUP_WIDTH
    tail = lambda n: pl.BlockSpec((None, kv_w, n), lambda i: (i // TILES_PER_SEQ, 0, 0))
    return pl.pallas_call(
        _in_proj_prompt_kernel,
        grid=(rows // tm,),
        in_specs=[row(D_MODEL), _const_spec((1, D_MODEL)), _const_spec((D_MODEL, IN_COLS)), tab, tab, tab,
                  _const_spec((CONV_WIDTH, C_CONV)), _const_spec((1, C_CONV))],
        out_specs=[pl.BlockSpec((N_SLABS, tm, LANES), lambda i: (0, i, 0)),
                   row(ATTN_WIDTH), row(kv_w), row(kv_w), row(kv_w), row(2 * D_MODEL),
                   tail(GROUPS[0][0]), tail(GROUPS[1][0]),
                   pl.BlockSpec((None, kv_w, tm), lambda i: (i // TILES_PER_SEQ, 0, i % TILES_PER_SEQ)),
                   pl.BlockSpec((None, CONV_HALO, C_CONV), lambda i: (i // TILES_PER_SEQ, 0, 0))],
        out_shape=[jax.ShapeDtypeStruct((N_SLABS, rows, LANES), F32),
                   jax.ShapeDtypeStruct((rows, ATTN_WIDTH), F32),
                   jax.ShapeDtypeStruct((rows, kv_w), F32),
                   jax.ShapeDtypeStruct((rows, kv_w), F32),
                   jax.ShapeDtypeStruct((rows, kv_w), F32),
                   jax.ShapeDtypeStruct((rows, 2 * D_MODEL), BF16),
                   jax.ShapeDtypeStruct((BATCH, kv_w, GROUPS[0][0]), F32),
                   jax.ShapeDtypeStruct((BATCH, kv_w, GROUPS[1][0]), F32),
                   jax.ShapeDtypeStruct((BATCH, kv_w, SEQ), F32),
                   jax.ShapeDtypeStruct((BATCH, CONV_HALO, C_CONV), F32)],
        scratch_shapes=[pltpu.VMEM((N_SLABS, CONV_HALO + tm, LANES), F32)],
        compiler_params=_params(1),
        name="in_proj_prompt",
    )(x2d, g, w, cos, sa, sb, wdw, bdw)


def _in_proj_sample_kernel(x_ref, g_ref, w_ref, cos_ref, sa_ref, sb_ref,
                           u_ref, qt_ref, ta_ref, tb_ref, tc_ref, gate_ref):
    qkv, glu, gates = _projector(x_ref, g_ref, w_ref, cos_ref, sa_ref, sb_ref)
    ch = GROUP_WIDTH
    for g, t_ref in enumerate((ta_ref, tb_ref, tc_ref)):
        q, k, v = qkv(g)
        qt_ref[g * ch:(g + 1) * ch, :] = q.T
        t_ref[0:ch, :] = k.T
        t_ref[ch:2 * ch, :] = v.T
    for c in range(0, C_CONV, ch):
        u = glu(c)
        for j in range(ch // LANES):
            u_ref[c // LANES + j] = u[:, j * LANES:(j + 1) * LANES]
    gates(gate_ref)


def _in_proj_sample(x2d, g, w, cos, sa, sb):
    n = x2d.shape[0]
    full = lambda r, c: pl.BlockSpec((r, c), lambda i: (0, 0))
    kv_w = 2 * GROUP_WIDTH
    return pl.pallas_call(
        _in_proj_sample_kernel,
        grid=(1,),
        in_specs=[full(n, D_MODEL), _const_spec((1, D_MODEL)), _const_spec((D_MODEL, IN_COLS)),
                  full(n, GROUP_WIDTH), full(n, GROUP_WIDTH), full(n, GROUP_WIDTH)],
        out_specs=[pl.BlockSpec((N_SLABS, n, LANES), lambda i: (0, 0, 0)),
                   full(ATTN_WIDTH, n), full(kv_w, n), full(kv_w, n), full(kv_w, n), full(n, 2 * D_MODEL)],
        out_shape=[jax.ShapeDtypeStruct((N_SLABS, n, LANES), F32),
                   jax.ShapeDtypeStruct((ATTN_WIDTH, n), F32),
                   jax.ShapeDtypeStruct((kv_w, n), F32),
                   jax.ShapeDtypeStruct((kv_w, n), F32),
                   jax.ShapeDtypeStruct((kv_w, n), F32),
                   jax.ShapeDtypeStruct((n, 2 * D_MODEL), BF16)],
        compiler_params=_params(1),
        name="in_proj_sample",
    )(x2d, g, w, cos, sa, sb)


ATTN_UNROLL = 4


def _combine_groups(lses, outs):
    mx = jnp.maximum(jnp.maximum(lses[0], lses[1]), lses[2])
    es = [jnp.exp(l - mx) for l in lses]
    num = es[0] * outs[0] + es[1] * outs[1] + es[2] * outs[2]
    return num / (es[0] + es[1] + es[2])


def _largest_divisor(n, cap):
    return max(d for d in range(1, cap + 1) if n % d == 0)


def _attn_kernel(qa_ref, qb_ref, qc_ref, ka_ref, va_ref, kb_ref, vb_ref, kc_ref, vc_ref,
                 o_ref, og_ref, lg_ref):
    lane = lax.broadcasted_iota(jnp.int32, (NK, LANES), 1)
    qi = lax.broadcasted_iota(jnp.int32, (NK, LANES), 0)
    first_head = lane < HEAD_DIM
    own_mask = lane <= qi
    prev_mask = lane >= qi
    nt = (((1,), (1,)), ((), ()))

    def blocks(g, q_ref, k_ref, v_ref, row_list):
        scores, values = [], []
        for rows, prev_rows in row_list:
            qb = q_ref[rows, :]
            keys = [k_ref[rows, :].astype(BF16)]
            vals = [v_ref[rows, :].astype(BF16)]
            masks = [own_mask]
            if prev_rows is not None:
                keys.append(k_ref[prev_rows, :].astype(BF16))
                vals.append(v_ref[prev_rows, :].astype(BF16))
                masks.append(prev_mask)
            values.append(jnp.concatenate(vals, axis=0) if len(vals) > 1 else vals[0])
            for head_mask in (first_head, jnp.logical_not(first_head)):
                qh = jnp.where(head_mask, qb, 0.0).astype(BF16)
                scores.append([jnp.where(mk, lax.dot_general(qh, kk, nt, preferred_element_type=F32), MASKED)
                               for kk, mk in zip(keys, masks)])
        probs, maxes = [], []
        for parts in scores:
            top = parts[0] if len(parts) == 1 else jnp.maximum(parts[0], parts[1])
            m = jnp.max(top, axis=-1, keepdims=True)
            es = [jnp.exp(s - m) for s in parts]
            probs.append((jnp.concatenate(es, axis=1) if len(es) > 1 else es[0]).astype(BF16))
            maxes.append(m)
        for b, (rows, _) in enumerate(row_list):
            vals = values[b]
            ones = jnp.ones((vals.shape[0], LANES), BF16)
            outs, lses = [], []
            for h in range(2):
                p = probs[2 * b + h]
                l = jnp.dot(p, ones, preferred_element_type=F32)
                outs.append(jnp.dot(p, vals, preferred_element_type=F32) / l)
                lses.append(maxes[2 * b + h] + jnp.log(l))
            og_ref[g, rows, :] = jnp.where(first_head, outs[0], outs[1])
            lg_ref[g, rows, :] = jnp.where(first_head, lses[0], lses[1])

    for g, (q_ref, k_ref, v_ref) in enumerate(((qa_ref, ka_ref, va_ref), (qb_ref, kb_ref, vb_ref),
                                               (qc_ref, kc_ref, vc_ref))):
        dil = GROUPS[g][1]
        span = NK * dil
        n_later = dil * (SEQ // span - 1)

        def rows_at(start, dil=dil):
            return pl.ds(start, NK) if dil == 1 else pl.ds(start, NK, stride=dil)

        n_first = _largest_divisor(dil, ATTN_UNROLL)

        def first(t, carry, g=g, q_ref=q_ref, k_ref=k_ref, v_ref=v_ref, rows_at=rows_at, n=n_first):
            blocks(g, q_ref, k_ref, v_ref, [(rows_at(t * n + j), None) for j in range(n)])
            return carry

        lax.fori_loop(0, dil // n_first, first, 0)
        if n_later:
            n_per = _largest_divisor(n_later, ATTN_UNROLL)

            def later(t, carry, g=g, q_ref=q_ref, k_ref=k_ref, v_ref=v_ref, rows_at=rows_at, dil=dil,
                      span=span, n=n_per):
                row_list = []
                for j in range(n):
                    i = t * n + j
                    start = i % dil + (1 + i // dil) * span
                    row_list.append((rows_at(start), rows_at(start - span)))
                blocks(g, q_ref, k_ref, v_ref, row_list)
                return carry

            lax.fori_loop(0, n_later // n_per, later, 0)

    tc = 256
    for t in range(0, SEQ, tc):
        rows = slice(t, t + tc)
        o_ref[rows, :] = _combine_groups([lg_ref[g, rows, :] for g in range(N_GROUPS)],
                                         [og_ref[g, rows, :] for g in range(N_GROUPS)]).astype(o_ref.dtype)


def _attention(q3, kva3, kvb3, kvc3):
    def col(c):
        return pl.BlockSpec((None, SEQ, LANES), lambda b, hp, c=c: (b, 0, c + hp))
    n_pairs = GROUP_WIDTH // LANES
    return pl.pallas_call(
        _attn_kernel,
        grid=(BATCH, n_pairs),
        in_specs=[col(0), col(n_pairs), col(2 * n_pairs),
                  col(0), col(n_pairs), col(0), col(n_pairs), col(0), col(n_pairs)],
        out_specs=pl.BlockSpec((None, SEQ, LANES), lambda b, hp: (b, 0, hp)),
        out_shape=jax.ShapeDtypeStruct((BATCH, SEQ, GROUP_WIDTH), BF16),
        scratch_shapes=[pltpu.VMEM((N_GROUPS, SEQ, LANES), F32), pltpu.VMEM((N_GROUPS, SEQ, LANES), F32)],
        compiler_params=_params(2),
        name="prompt_attention",
    )(q3, q3, q3, kva3, kva3, kvb3, kvb3, kvc3, kvc3)


def _merge(c, x, ob, gate_ref, lng_ref, lnb_ref, wco_ref, wao_ref, wo_ref):
    mu = jnp.mean(c, axis=-1, keepdims=True)
    d = c - mu
    var = jnp.mean(d * d, axis=-1, keepdims=True)
    y = d * lax.rsqrt(var + NORM_EPS) * lng_ref[...] + lnb_ref[...]
    a_out = jnp.dot(jax.nn.silu(y).astype(BF16), wco_ref[...], preferred_element_type=F32)
    b_out = jnp.dot(ob, wao_ref[...], preferred_element_type=F32)
    merged = (gate_ref[:, 0:D_MODEL].astype(F32) * a_out
              + gate_ref[:, D_MODEL:2 * D_MODEL].astype(F32) * b_out)
    return x + jnp.dot(merged.astype(BF16), wo_ref[...], preferred_element_type=F32)


MIX_TM = 512


def _prompt_mix_kernel(c_ref, x_ref, o_ref, gate_ref, lng_ref, lnb_ref, wco_ref, wao_ref, wo_ref, x1_ref):
    c = jnp.concatenate([c_ref[lc] for lc in range(N_SLABS)], axis=-1)
    x1_ref[...] = _merge(c, x_ref[...], o_ref[...], gate_ref, lng_ref, lnb_ref, wco_ref, wao_ref, wo_ref)


def _prompt_mix(c_slabs, x2d, o2d, gate2d, lng, lnb, wco, wao, wo):
    tm = MIX_TM
    rows = x2d.shape[0]
    row = lambda n: pl.BlockSpec((tm, n), lambda i: (i, 0))
    return pl.pallas_call(
        _prompt_mix_kernel,
        grid=(rows // tm,),
        in_specs=[pl.BlockSpec((N_SLABS, tm, LANES), lambda i: (0, i, 0)),
                  row(D_MODEL), row(GROUP_WIDTH), row(2 * D_MODEL),
                  _const_spec((1, C_CONV)), _const_spec((1, C_CONV)),
                  _const_spec((C_CONV, D_MODEL)), _const_spec((GROUP_WIDTH, D_MODEL)),
                  _const_spec((D_MODEL, D_MODEL))],
        out_specs=row(D_MODEL),
        out_shape=jax.ShapeDtypeStruct((rows, D_MODEL), F32),
        compiler_params=_params(1),
        name="prompt_mix",
    )(c_slabs, x2d, o2d, gate2d, lng, lnb, wco, wao, wo)


def _sample_mix_kernel(c_ref, x_ref, oa_ref, ob_ref, oc_ref, la_ref, lb_ref, lc_ref, gate_ref,
                       lng_ref, lnb_ref, wco_ref, wao_ref, wo_ref, x1_ref):
    ob = _combine_groups([r[...].T for r in (la_ref, lb_ref, lc_ref)],
                         [r[...].T for r in (oa_ref, ob_ref, oc_ref)]).astype(BF16)
    x1_ref[...] = _merge(c_ref[...], x_ref[...], ob, gate_ref, lng_ref, lnb_ref, wco_ref, wao_ref, wo_ref)


def _sample_mix(c, x, ogs, lgs, gate, lng, lnb, wco, wao, wo):
    rows = x.shape[0]
    full = lambda n: _const_spec((rows, n))
    per_group = _const_spec((GROUP_WIDTH, rows))
    return pl.pallas_call(
        _sample_mix_kernel,
        grid=(1,),
        in_specs=[full(C_CONV), full(D_MODEL)] + [per_group] * (2 * N_GROUPS) + [full(2 * D_MODEL),
                  _const_spec((1, C_CONV)), _const_spec((1, C_CONV)),
                  _const_spec((C_CONV, D_MODEL)), _const_spec((GROUP_WIDTH, D_MODEL)),
                  _const_spec((D_MODEL, D_MODEL))],
        out_specs=pl.BlockSpec((rows, D_MODEL), lambda i: (0, 0)),
        out_shape=jax.ShapeDtypeStruct((rows, D_MODEL), F32),
        compiler_params=_params(1),
        name="sample_mix",
    )(c, x, *ogs, *lgs, gate, lng, lnb, wco, wao, wo)


FF_CHUNK = 256


def _ffn_kernel(x_ref, p_ref, gf_ref, wfi_ref, wfo_ref, gp_ref, wpg_ref, wpp_ref, gfin_ref, y_ref, acc_ref):
    x1 = x_ref[...]
    hb = _rms(x1, gf_ref[...]).astype(BF16)
    acc_ref[...] = jnp.zeros_like(acc_ref)
    for c in range(0, D_FF, FF_CHUNK):
        gch = jnp.dot(hb, wfi_ref[:, c:c + FF_CHUNK], preferred_element_type=F32)
        uch = jnp.dot(hb, wfi_ref[:, D_FF + c:D_FF + c + FF_CHUNK], preferred_element_type=F32)
        act = (jax.nn.silu(gch) * uch).astype(BF16)
        acc_ref[...] += jnp.dot(act, wfo_ref[c:c + FF_CHUNK, :], preferred_element_type=F32)
    x2 = x1 + acc_ref[...]
    hp = _rms(x2, gp_ref[...]).astype(BF16)
    gate = jax.nn.sigmoid(jnp.dot(hp, wpg_ref[...], preferred_element_type=F32))
    pe = jnp.dot(p_ref[...].astype(BF16), wpp_ref[...], preferred_element_type=F32)
    x3 = x2 + gate * pe
    y_ref[...] = _rms(x3, gfin_ref[...])


def _ffn(x1, p, gf, wfi, wfo, gp, wpg, wpp, gfin, tm):
    rows = x1.shape[0]
    row = lambda n: pl.BlockSpec((tm, n), lambda i: (i, 0))
    return pl.pallas_call(
        _ffn_kernel,
        grid=(rows // tm,),
        in_specs=[row(D_MODEL), row(PLE_DIM), _const_spec((1, D_MODEL)),
                  _const_spec((D_MODEL, 2 * D_FF)), _const_spec((D_FF, D_MODEL)),
                  _const_spec((1, D_MODEL)), _const_spec((D_MODEL, D_MODEL)),
                  _const_spec((PLE_DIM, D_MODEL)), _const_spec((1, D_MODEL))],
        out_specs=row(D_MODEL),
        out_shape=jax.ShapeDtypeStruct((rows, D_MODEL), F32),
        scratch_shapes=[pltpu.VMEM((tm, D_MODEL), F32)],
        compiler_params=_params(1),
        name="ffn_ple",
    )(x1, p, gf, wfi, wfo, gp, wpg, wpp, gfin)


SAMPLE_CONV_BLOCK = 32


def _sample_conv_kernel(state_ref, u_ref, wdw_ref, bdw_ref, c_ref, new_ref):
    ctx = CONV_WIDTH - 1
    for lc in range(N_SLABS):
        cols = slice(lc * LANES, (lc + 1) * LANES)
        u = u_ref[lc]
        acc = u * wdw_ref[ctx:ctx + 1, cols] + bdw_ref[:, cols]
        for k in range(ctx):
            acc = acc + state_ref[k, :, cols] * wdw_ref[k:k + 1, cols]
        c_ref[:, cols] = acc
        new_ref[ctx - 1, :, cols] = u
    for k in range(ctx - 1):
        new_ref[k] = state_ref[k + 1]


def _sample_conv(state_t, u_slabs, wdw, bdw):
    ctx, n, _ = state_t.shape
    sb = SAMPLE_CONV_BLOCK
    return pl.pallas_call(
        _sample_conv_kernel,
        grid=(n // sb,),
        in_specs=[pl.BlockSpec((ctx, sb, C_CONV), lambda i: (0, i, 0)),
                  pl.BlockSpec((N_SLABS, sb, LANES), lambda i: (0, i, 0)),
                  _const_spec((CONV_WIDTH, C_CONV)), _const_spec((1, C_CONV))],
        out_specs=[pl.BlockSpec((sb, C_CONV), lambda i: (i, 0)),
                   pl.BlockSpec((ctx, sb, C_CONV), lambda i: (0, i, 0))],
        out_shape=[jax.ShapeDtypeStruct((n, C_CONV), F32),
                   jax.ShapeDtypeStruct((ctx, n, C_CONV), F32)],
        compiler_params=_params(1),
        name="sample_conv",
    )(state_t, u_slabs, wdw, bdw)


def _sample_cache_kernel(q_ref, new_ref, cache_ref, out_ref, o_ref, lse_ref, *, dil):
    sb, _, _, _, length = cache_ref.shape
    step = pl.program_id(0)

    @pl.when(step == 0)
    def _():
        o_ref[...] = jnp.zeros_like(o_ref)
        lse_ref[...] = jnp.zeros_like(lse_ref)

    pos = lax.broadcasted_iota(jnp.int32, (1, length), 1)
    used = (pos & (dil - 1)) == 0
    lane = lax.broadcasted_iota(jnp.int32, (HEAD_DIM, LANES), 1)
    last = lane == LANES - 1

    def sample(s, carry):
        mine = lane == step * sb + s

        def pick(ref, r0):
            return jnp.sum(jnp.where(mine, ref[r0:r0 + HEAD_DIM, :], 0.0), axis=1, keepdims=True)

        def shift(kv, h, new):
            rolled = pltpu.roll(cache_ref[s, kv, h], length - 1, 1)
            out_ref[s, kv, h] = rolled
            out_ref[s, kv, h, :, length - LANES:] = jnp.where(last, new, rolled[:, length - LANES:])

        heads = range(H_G)
        q = [pick(q_ref, h * HEAD_DIM) for h in heads]
        k_new = [pick(new_ref, h * HEAD_DIM) for h in heads]
        v_new = [pick(new_ref, GROUP_WIDTH + h * HEAD_DIM) for h in heads]
        shift(0, 0, k_new[0])
        shift(1, 0, v_new[0])
        sc = [jnp.where(used, jnp.sum(cache_ref[s, 0, h] * q[h], axis=0, keepdims=True), MASKED) for h in heads]
        s_new = [jnp.sum(k_new[h] * q[h], axis=0, keepdims=True) for h in heads]
        m = [jnp.maximum(jnp.max(sc[h], axis=1, keepdims=True), s_new[h]) for h in heads]
        shift(0, 1, k_new[1])
        shift(1, 1, v_new[1])
        p = [jnp.exp(sc[h] - m[h]) for h in heads]
        p_new = [jnp.exp(s_new[h] - m[h]) for h in heads]
        l = [jnp.sum(p[h], axis=1, keepdims=True) + p_new[h] for h in heads]
        shift(0, 2, k_new[2])
        shift(1, 2, v_new[2])
        o = [(jnp.sum(cache_ref[s, 1, h] * p[h], axis=1, keepdims=True) + v_new[h] * p_new[h]) / l[h]
             for h in heads]
        shift(0, 3, k_new[3])
        shift(1, 3, v_new[3])
        for h in heads:
            rows = slice(h * HEAD_DIM, (h + 1) * HEAD_DIM)
            o_ref[rows, :] = jnp.where(mine, o[h], o_ref[rows, :])
            lse_ref[rows, :] = jnp.where(mine, m[h] + jnp.log(l[h]), lse_ref[rows, :])
        return carry

    lax.fori_loop(0, sb, sample, 0)


SAMPLE_CACHE_BLOCK_BYTES = 4 * 1024 * 1024


def _sample_cache(q_t, new_t, cache_t, g):
    n, _, _, _, length = cache_t.shape
    sb = max(1, SAMPLE_CACHE_BLOCK_BYTES // (2 * H_G * HEAD_DIM * length * 4))
    big = pl.BlockSpec((sb, 2, H_G, HEAD_DIM, length), lambda i: (i, 0, 0, 0, 0))
    res = pl.BlockSpec((GROUP_WIDTH, n), lambda i: (0, 0))
    return pl.pallas_call(
        functools.partial(_sample_cache_kernel, dil=GROUPS[g][1]),
        grid=(n // sb,),
        in_specs=[pl.BlockSpec((GROUP_WIDTH, n), lambda i, g=g: (g, 0)),
                  pl.BlockSpec((2 * GROUP_WIDTH, n), lambda i: (0, 0)), big],
        out_specs=[big, res, res],
        out_shape=[jax.ShapeDtypeStruct(cache_t.shape, F32),
                   jax.ShapeDtypeStruct((GROUP_WIDTH, n), F32),
                   jax.ShapeDtypeStruct((GROUP_WIDTH, n), F32)],
        compiler_params=_params(1),
        name="sample_cache_" + "abc"[g],
    )(q_t, new_t, cache_t)


def _rope_tables(pos):
    half = ROT_DIM // 2
    inv_freq = ROPE_THETA ** (-jnp.arange(half, dtype=F32) / half)
    ang = pos.astype(F32)[:, None] * inv_freq[None, :]
    cos, sin = jnp.cos(ang), jnp.sin(ang)
    n = pos.shape[0]
    zeros = lambda w: jnp.zeros((n, w), F32)
    cos_h = jnp.concatenate([cos, cos, jnp.ones((n, HEAD_DIM - ROT_DIM), F32)], axis=-1)
    sa_h = jnp.concatenate([-sin, zeros(HEAD_DIM - half)], axis=-1)
    sb_h = jnp.concatenate([zeros(half), sin, zeros(HEAD_DIM - ROT_DIM)], axis=-1)
    return tuple(jnp.tile(t, (1, H_G)) for t in (cos_h, sa_h, sb_h))


def _position_major(t):
    b, _, length = t.shape
    return jnp.transpose(t.reshape(b, 2, H_G, HEAD_DIM, length), (0, 4, 1, 2, 3))[None]


def kernel(x_prompt, x_sample, state_conv, cache_win_a, cache_win_b, cache_win_c, p_prompt, p_sample,
           w_in, g_mix, w_dw, b_dw, ln_g, ln_b, w_conv_out, w_attn_out, w_o, g_ffn, w_ffn_in, w_ffn_out,
           g_ple, w_ple_gate, w_ple_proj, g_final):
    assert w_in.shape[0] == 1, "single-layer step"
    w_in_b = w_in[0].astype(BF16)
    wco = w_conv_out[0].astype(BF16)
    wao = w_attn_out[0].astype(BF16)
    wo = w_o[0].astype(BF16)
    wfi = w_ffn_in[0].astype(BF16)
    wfo = w_ffn_out[0].astype(BF16)
    wpg = w_ple_gate[0].astype(BF16)
    wpp = w_ple_proj[0].astype(BF16)
    gfin = g_final.reshape(1, D_MODEL)
    n_prompt = BATCH * SEQ
    ctx = CONV_WIDTH - 1

    xp = x_prompt.reshape(n_prompt, D_MODEL)
    tabs_p = _rope_tables(jnp.arange(SEQ, dtype=jnp.int32))
    c_slabs, q, kva, kvb, kvc, gates, tail_a, tail_b, tail_c, u_tail = _in_proj_prompt(
        xp, g_mix, w_in_b, *tabs_p, w_dw[0], b_dw)
    per_batch = lambda t: t.reshape(BATCH, SEQ, t.shape[-1])
    o = _attention(per_batch(q), per_batch(kva), per_batch(kvb), per_batch(kvc))
    x1 = _prompt_mix(c_slabs, xp, o.reshape(n_prompt, GROUP_WIDTH), gates, ln_g, ln_b, wco, wao, wo)
    y_prompt = _ffn(x1, p_prompt.reshape(n_prompt, PLE_DIM), g_ffn, wfi, wfo,
                    g_ple, wpg, wpp, gfin, PROMPT_TM).reshape(BATCH, SEQ, D_MODEL)
    new_conv_prompt = u_tail[None, :, CONV_HALO - ctx:, :]

    n_s = DEC_BATCH
    xs = x_sample.reshape(n_s, D_MODEL)
    tabs_s = tuple(jnp.broadcast_to(t, (n_s, GROUP_WIDTH))
                   for t in _rope_tables(jnp.full((1,), PAST_LEN, jnp.int32)))
    u_s, q_t, new_a, new_b, new_c, gates_s = _in_proj_sample(xs, g_mix, w_in_b, *tabs_s)
    c_s, new_conv_t = _sample_conv(jnp.transpose(state_conv[0], (1, 0, 2)), u_s, w_dw[0], b_dw)
    new_wins, ogs, lgs = [], [], []
    for g, (new_t, cache) in enumerate(zip((new_a, new_b, new_c), (cache_win_a, cache_win_b, cache_win_c))):
        cache_t = jnp.transpose(cache[0], (0, 2, 3, 4, 1))
        shifted, o_g, lse_g = _sample_cache(q_t, new_t, cache_t, g)
        new_wins.append(jnp.transpose(shifted, (0, 4, 1, 2, 3))[None])
        ogs.append(o_g)
        lgs.append(lse_g)
    x1_s = _sample_mix(c_s, xs, ogs, lgs, gates_s, ln_g, ln_b, wco, wao, wo)
    y_sample = _ffn(x1_s, p_sample.reshape(n_s, PLE_DIM), g_ffn, wfi, wfo, g_ple, wpg, wpp, gfin,
                    n_s).reshape(n_s, 1, D_MODEL)
    new_conv_sample = jnp.transpose(new_conv_t, (1, 0, 2))[None]

    return (y_prompt, y_sample, new_conv_prompt, _position_major(tail_a), _position_major(tail_b),
            _position_major(tail_c), new_conv_sample, new_wins[0], new_wins[1], new_wins[2])
```

```python
import functools

import jax
import jax.numpy as jnp
from jax import lax
from jax.experimental import pallas as pl
from jax.experimental.pallas import tpu as pltpu

D_MODEL = 1024
BATCH = 8
SEQ = 2048
DEC_BATCH = 128
PAST_LEN = 8192
HEAD_DIM = 64
GROUPS = ((128, 1), (512, 4), (2048, 16))
H_G = 4
N_GROUPS = len(GROUPS)
GROUP_WIDTH = H_G * HEAD_DIM
ATTN_WIDTH = N_GROUPS * GROUP_WIDTH
ROT_DIM = HEAD_DIM // 4
ROPE_THETA = 500000.0
C_CONV = D_MODEL
CONV_WIDTH = 31
D_FF = 2816
PLE_DIM = 256
NORM_EPS = 1e-6
NK = 128
Q_OFF = 2 * C_CONV
K_OFF = Q_OFF + ATTN_WIDTH
V_OFF = K_OFF + ATTN_WIDTH
GATE_OFF = V_OFF + ATTN_WIDTH
IN_COLS = GATE_OFF + 2 * D_MODEL

F32 = jnp.float32
BF16 = jnp.bfloat16
LANES = 128
SUBLANES = 8
N_SLABS = C_CONV // LANES
MASKED = -1e30
VMEM_LIMIT = 52 * 1024 * 1024


def _const_spec(shape):
    return pl.BlockSpec(shape, lambda *_: (0,) * len(shape), pipeline_mode=pl.Buffered(1))


def _params(n_axes):
    return pltpu.CompilerParams(dimension_semantics=("arbitrary",) * n_axes,
                                vmem_limit_bytes=VMEM_LIMIT)


def _rms(x, g):
    return x * lax.rsqrt(jnp.mean(x * x, axis=-1, keepdims=True) + NORM_EPS) * g


PROMPT_TM = 512
TILES_PER_SEQ = SEQ // PROMPT_TM
CONV_HALO = 32
CONV_PITCH = 4
CONV_GROUP = CONV_PITCH * SUBLANES
CONV_UNROLL = 1


def _projector(x_ref, g_ref, w_ref, cos_ref, sa_ref, sb_ref):
    hb = _rms(x_ref[...], g_ref[...]).astype(BF16)
    ch = GROUP_WIDTH

    def mm(c0):
        return jnp.dot(hb, w_ref[:, c0:c0 + ch], preferred_element_type=F32)

    def glu(c):
        return mm(c) * jax.nn.sigmoid(mm(C_CONV + c))

    def gates(gate_ref, lo=0, hi=2 * D_MODEL // ch):
        for c in range(lo * ch, hi * ch, ch):
            gate_ref[:, c:c + ch] = jax.nn.sigmoid(mm(GATE_OFF + c)).astype(BF16)

    def rope(z):
        return (z * cos_ref[...] + pltpu.roll(z, ch - ROT_DIM // 2, 1) * sa_ref[...]
                + pltpu.roll(z, ROT_DIM // 2, 1) * sb_ref[...])

    def qkv(g):
        c = g * ch
        return rope(mm(Q_OFF + c)) * (HEAD_DIM ** -0.5), rope(mm(K_OFF + c)), mm(V_OFF + c)

    return qkv, glu, gates


def _conv_slab(win_ref, lc, wdw_ref, bdw_ref, c_ref, n_rows):
    first = CONV_HALO - (CONV_WIDTH - 1)
    cols = slice(lc * LANES, (lc + 1) * LANES)
    n_acc = CONV_UNROLL * CONV_PITCH
    offs = [(i // CONV_PITCH) * CONV_GROUP + i % CONV_PITCH for i in range(n_acc)]
    bias = jnp.broadcast_to(bdw_ref[:, cols], (SUBLANES, LANES))
    for base in range(0, n_rows, CONV_UNROLL * CONV_GROUP):
        accs = [bias] * n_acc
        for k in range(CONV_WIDTH):
            wk = wdw_ref[k:k + 1, cols]
            for i in range(n_acc):
                tap = win_ref[lc, pl.ds(base + first + k + offs[i], SUBLANES, stride=CONV_PITCH), :]
                accs[i] = accs[i] + tap * wk
        for i in range(n_acc):
            c_ref[lc, pl.ds(base + offs[i], SUBLANES, stride=CONV_PITCH), :] = accs[i]


def _in_proj_prompt_kernel(x_ref, g_ref, w_ref, cos_ref, sa_ref, sb_ref, wdw_ref, bdw_ref,
                           c_ref, q_ref, kva_ref, kvb_ref, kvc_ref, gate_ref, ta_ref, tb_ref, tc_ref, ut_ref,
                           win_ref):
    tm = PROMPT_TM
    ch = GROUP_WIDTH

    @pl.when(pl.program_id(0) == 0)
    def _():
        win_ref[:, tm:tm + CONV_HALO, :] = jnp.zeros((N_SLABS, CONV_HALO, LANES), F32)

    qkv, glu, gates = _projector(x_ref, g_ref, w_ref, cos_ref, sa_ref, sb_ref)
    first_tile = pl.program_id(0) % TILES_PER_SEQ == 0
    def head_group(g):
        kv_ref, t_ref = ((kva_ref, ta_ref), (kvb_ref, tb_ref), (kvc_ref, tc_ref))[g]
        q, k, v = qkv(g)
        q_ref[:, g * ch:(g + 1) * ch] = q
        kv_ref[:, 0:ch] = k
        kv_ref[:, ch:2 * ch] = v
        n = t_ref.shape[1]
        t_ref[0:ch, :] = k[tm - n:, :].T
        t_ref[ch:2 * ch, :] = v[tm - n:, :].T

    def glu_chunk(i):
        c = i * ch
        u = glu(c)
        ut_ref[:, c:c + ch] = u[tm - CONV_HALO:, :]
        for j in range(ch // LANES):
            lc = c // LANES + j
            win_ref[lc, 0:CONV_HALO, :] = jnp.where(first_tile, 0.0, win_ref[lc, tm:tm + CONV_HALO, :])
            win_ref[lc, CONV_HALO:CONV_HALO + tm, :] = u[:, j * LANES:(j + 1) * LANES]

    n_gate = 2 * D_MODEL // ch
    after_slab = [
        lambda: glu_chunk(1), lambda: head_group(0), lambda: glu_chunk(2), lambda: head_group(1),
        lambda: glu_chunk(3), lambda: head_group(2),
        lambda: gates(gate_ref, 0, n_gate // 2), lambda: gates(gate_ref, n_gate // 2, n_gate)]
    glu_chunk(0)
    for lc in range(N_SLABS):
        _conv_slab(win_ref, lc, wdw_ref, bdw_ref, c_ref, tm)
        after_slab[lc]()


def _in_proj_prompt(x2d, g, w, cos, sa, sb, wdw, bdw):
    tm = PROMPT_TM
    rows = x2d.shape[0]
    assert GROUPS[1][0] == tm and GROUPS[2][0] == SEQ and GROUPS[0][0] <= tm
    row = lambda n: pl.BlockSpec((tm, n), lambda i: (i, 0))
    tab = pl.BlockSpec((tm, GROUP_WIDTH), lambda i: (i % TILES_PER_SEQ, 0))
    kv_w = 2 * GROUP_WIDTH
    tail = lambda n: pl.BlockSpec((None, kv_w, n), lambda i: (i // TILES_PER_SEQ, 0, 0))
    return pl.pallas_call(
        _in_proj_prompt_kernel,
        grid=(rows // tm,),
        in_specs=[row(D_MODEL), _const_spec((1, D_MODEL)), _const_spec((D_MODEL, IN_COLS)), tab, tab, tab,
                  _const_spec((CONV_WIDTH, C_CONV)), _const_spec((1, C_CONV))],
        out_specs=[pl.BlockSpec((N_SLABS, tm, LANES), lambda i: (0, i, 0)),
                   row(ATTN_WIDTH), row(kv_w), row(kv_w), row(kv_w), row(2 * D_MODEL),
                   tail(GROUPS[0][0]), tail(GROUPS[1][0]),
                   pl.BlockSpec((None, kv_w, tm), lambda i: (i // TILES_PER_SEQ, 0, i % TILES_PER_SEQ)),
                   pl.BlockSpec((None, CONV_HALO, C_CONV), lambda i: (i // TILES_PER_SEQ, 0, 0))],
        out_shape=[jax.ShapeDtypeStruct((N_SLABS, rows, LANES), F32),
                   jax.ShapeDtypeStruct((rows, ATTN_WIDTH), F32),
                   jax.ShapeDtypeStruct((rows, kv_w), F32),
                   jax.ShapeDtypeStruct((rows, kv_w), F32),
                   jax.ShapeDtypeStruct((rows, kv_w), F32),
                   jax.ShapeDtypeStruct((rows, 2 * D_MODEL), BF16),
                   jax.ShapeDtypeStruct((BATCH, kv_w, GROUPS[0][0]), F32),
                   jax.ShapeDtypeStruct((BATCH, kv_w, GROUPS[1][0]), F32),
                   jax.ShapeDtypeStruct((BATCH, kv_w, SEQ), F32),
                   jax.ShapeDtypeStruct((BATCH, CONV_HALO, C_CONV), F32)],
        scratch_shapes=[pltpu.VMEM((N_SLABS, CONV_HALO + tm, LANES), F32)],
        compiler_params=_params(1),
        name="in_proj_prompt",
    )(x2d, g, w, cos, sa, sb, wdw, bdw)


def _in_proj_sample_kernel(x_ref, g_ref, w_ref, cos_ref, sa_ref, sb_ref,
                           u_ref, qt_ref, ta_ref, tb_ref, tc_ref, gate_ref):
    qkv, glu, gates = _projector(x_ref, g_ref, w_ref, cos_ref, sa_ref, sb_ref)
    ch = GROUP_WIDTH
    for g, t_ref in enumerate((ta_ref, tb_ref, tc_ref)):
        q, k, v = qkv(g)
        qt_ref[g * ch:(g + 1) * ch, :] = q.T
        t_ref[0:ch, :] = k.T
        t_ref[ch:2 * ch, :] = v.T
    for c in range(0, C_CONV, ch):
        u = glu(c)
        for j in range(ch // LANES):
            u_ref[c // LANES + j] = u[:, j * LANES:(j + 1) * LANES]
    gates(gate_ref)


def _in_proj_sample(x2d, g, w, cos, sa, sb):
    n = x2d.shape[0]
    full = lambda r, c: pl.BlockSpec((r, c), lambda i: (0, 0))
    kv_w = 2 * GROUP_WIDTH
    return pl.pallas_call(
        _in_proj_sample_kernel,
        grid=(1,),
        in_specs=[full(n, D_MODEL), _const_spec((1, D_MODEL)), _const_spec((D_MODEL, IN_COLS)),
                  full(n, GROUP_WIDTH), full(n, GROUP_WIDTH), full(n, GROUP_WIDTH)],
        out_specs=[pl.BlockSpec((N_SLABS, n, LANES), lambda i: (0, 0, 0)),
                   full(ATTN_WIDTH, n), full(kv_w, n), full(kv_w, n), full(kv_w, n), full(n, 2 * D_MODEL)],
        out_shape=[jax.ShapeDtypeStruct((N_SLABS, n, LANES), F32),
                   jax.ShapeDtypeStruct((ATTN_WIDTH, n), F32),
                   jax.ShapeDtypeStruct((kv_w, n), F32),
                   jax.ShapeDtypeStruct((kv_w, n), F32),
                   jax.ShapeDtypeStruct((kv_w, n), F32),
                   jax.ShapeDtypeStruct((n, 2 * D_MODEL), BF16)],
        compiler_params=_params(1),
        name="in_proj_sample",
    )(x2d, g, w, cos, sa, sb)


ATTN_UNROLL = 4


def _combine_groups(lses, outs):
    mx = jnp.maximum(jnp.maximum(lses[0], lses[1]), lses[2])
    es = [jnp.exp(l - mx) for l in lses]
    num = es[0] * outs[0] + es[1] * outs[1] + es[2] * outs[2]
    return num / (es[0] + es[1] + es[2])


def _largest_divisor(n, cap):
    return max(d for d in range(1, cap + 1) if n % d == 0)


def _attn_kernel(qa_ref, qb_ref, qc_ref, ka_ref, va_ref, kb_ref, vb_ref, kc_ref, vc_ref,
                 o_ref, og_ref, lg_ref):
    lane = lax.broadcasted_iota(jnp.int32, (NK, LANES), 1)
    qi = lax.broadcasted_iota(jnp.int32, (NK, LANES), 0)
    first_head = lane < HEAD_DIM
    own_mask = lane <= qi
    prev_mask = lane >= qi
    nt = (((1,), (1,)), ((), ()))

    def blocks(g, q_ref, k_ref, v_ref, row_list):
        scores, values = [], []
        for rows, prev_rows in row_list:
            qb = q_ref[rows, :]
            keys = [k_ref[rows, :].astype(BF16)]
            vals = [v_ref[rows, :].astype(BF16)]
            masks = [own_mask]
            if prev_rows is not None:
                keys.append(k_ref[prev_rows, :].astype(BF16))
                vals.append(v_ref[prev_rows, :].astype(BF16))
                masks.append(prev_mask)
            values.append(jnp.concatenate(vals, axis=0) if len(vals) > 1 else vals[0])
            for head_mask in (first_head, jnp.logical_not(first_head)):
                qh = jnp.where(head_mask, qb, 0.0).astype(BF16)
                scores.append([jnp.where(mk, lax.dot_general(qh, kk, nt, preferred_element_type=F32), MASKED)
                               for kk, mk in zip(keys, masks)])
        probs, maxes = [], []
        for parts in scores:
            top = parts[0] if len(parts) == 1 else jnp.maximum(parts[0], parts[1])
            m = jnp.max(top, axis=-1, keepdims=True)
            es = [jnp.exp(s - m) for s in parts]
            probs.append((jnp.concatenate(es, axis=1) if len(es) > 1 else es[0]).astype(BF16))
            maxes.append(m)
        for b, (rows, _) in enumerate(row_list):
            vals = values[b]
            ones = jnp.ones((vals.shape[0], LANES), BF16)
            outs, lses = [], []
            for h in range(2):
                p = probs[2 * b + h]
                l = jnp.dot(p, ones, preferred_element_type=F32)
                outs.append(jnp.dot(p, vals, preferred_element_type=F32) / l)
                lses.append(maxes[2 * b + h] + jnp.log(l))
            og_ref[g, rows, :] = jnp.where(first_head, outs[0], outs[1])
            lg_ref[g, rows, :] = jnp.where(first_head, lses[0], lses[1])

    for g, (q_ref, k_ref, v_ref) in enumerate(((qa_ref, ka_ref, va_ref), (qb_ref, kb_ref, vb_ref),
                                               (qc_ref, kc_ref, vc_ref))):
        dil = GROUPS[g][1]
        span = NK * dil
        n_later = dil * (SEQ // span - 1)

        def rows_at(start, dil=dil):
            return pl.ds(start, NK) if dil == 1 else pl.ds(start, NK, stride=dil)

        n_first = _largest_divisor(dil, ATTN_UNROLL)

        def first(t, carry, g=g, q_ref=q_ref, k_ref=k_ref, v_ref=v_ref, rows_at=rows_at, n=n_first):
            blocks(g, q_ref, k_ref, v_ref, [(rows_at(t * n + j), None) for j in range(n)])
            return carry

        lax.fori_loop(0, dil // n_first, first, 0)
        if n_later:
            n_per = _largest_divisor(n_later, ATTN_UNROLL)

            def later(t, carry, g=g, q_ref=q_ref, k_ref=k_ref, v_ref=v_ref, rows_at=rows_at, dil=dil,
                      span=span, n=n_per):
                row_list = []
                for j in range(n):
                    i = t * n + j
                    start = i % dil + (1 + i // dil) * span
                    row_list.append((rows_at(start), rows_at(start - span)))
                blocks(g, q_ref, k_ref, v_ref, row_list)
                return carry

            lax.fori_loop(0, n_later // n_per, later, 0)

    tc = 256
    for t in range(0, SEQ, tc):
        rows = slice(t, t + tc)
        o_ref[rows, :] = _combine_groups([lg_ref[g, rows, :] for g in range(N_GROUPS)],
                                         [og_ref[g, rows, :] for g in range(N_GROUPS)]).astype(o_ref.dtype)


def _attention(q3, kva3, kvb3, kvc3):
    def col(c):
        return pl.BlockSpec((None, SEQ, LANES), lambda b, hp, c=c: (b, 0, c + hp))
    n_pairs = GROUP_WIDTH // LANES
    return pl.pallas_call(
        _attn_kernel,
        grid=(BATCH, n_pairs),
        in_specs=[col(0), col(n_pairs), col(2 * n_pairs),
                  col(0), col(n_pairs), col(0), col(n_pairs), col(0), col(n_pairs)],
        out_specs=pl.BlockSpec((None, SEQ, LANES), lambda b, hp: (b, 0, hp)),
        out_shape=jax.ShapeDtypeStruct((BATCH, SEQ, GROUP_WIDTH), BF16),
        scratch_shapes=[pltpu.VMEM((N_GROUPS, SEQ, LANES), F32), pltpu.VMEM((N_GROUPS, SEQ, LANES), F32)],
        compiler_params=_params(2),
        name="prompt_attention",
    )(q3, q3, q3, kva3, kva3, kvb3, kvb3, kvc3, kvc3)


def _merge(c, x, ob, gate_ref, lng_ref, lnb_ref, wco_ref, wao_ref, wo_ref):
    mu = jnp.mean(c, axis=-1, keepdims=True)
    d = c - mu
    var = jnp.mean(d * d, axis=-1, keepdims=True)
    y = d * lax.rsqrt(var + NORM_EPS) * lng_ref[...] + lnb_ref[...]
    a_out = jnp.dot(jax.nn.silu(y).astype(BF16), wco_ref[...], preferred_element_type=F32)
    b_out = jnp.dot(ob, wao_ref[...], preferred_element_type=F32)
    merged = (gate_ref[:, 0:D_MODEL].astype(F32) * a_out
              + gate_ref[:, D_MODEL:2 * D_MODEL].astype(F32) * b_out)
    return x + jnp.dot(merged.astype(BF16), wo_ref[...], preferred_element_type=F32)


MIX_TM = 512


def _prompt_mix_kernel(c_ref, x_ref, o_ref, gate_ref, lng_ref, lnb_ref, wco_ref, wao_ref, wo_ref, x1_ref):
    c = jnp.concatenate([c_ref[lc] for lc in range(N_SLABS)], axis=-1)
    x1_ref[...] = _merge(c, x_ref[...], o_ref[...], gate_ref, lng_ref, lnb_ref, wco_ref, wao_ref, wo_ref)


def _prompt_mix(c_slabs, x2d, o2d, gate2d, lng, lnb, wco, wao, wo):
    tm = MIX_TM
    rows = x2d.shape[0]
    row = lambda n: pl.BlockSpec((tm, n), lambda i: (i, 0))
    return pl.pallas_call(
        _prompt_mix_kernel,
        grid=(rows // tm,),
        in_specs=[pl.BlockSpec((N_SLABS, tm, LANES), lambda i: (0, i, 0)),
                  row(D_MODEL), row(GROUP_WIDTH), row(2 * D_MODEL),
                  _const_spec((1, C_CONV)), _const_spec((1, C_CONV)),
                  _const_spec((C_CONV, D_MODEL)), _const_spec((GROUP_WIDTH, D_MODEL)),
                  _const_spec((D_MODEL, D_MODEL))],
        out_specs=row(D_MODEL),
        out_shape=jax.ShapeDtypeStruct((rows, D_MODEL), F32),
        compiler_params=_params(1),
        name="prompt_mix",
    )(c_slabs, x2d, o2d, gate2d, lng, lnb, wco, wao, wo)


def _sample_mix_kernel(c_ref, x_ref, oa_ref, ob_ref, oc_ref, la_ref, lb_ref, lc_ref, gate_ref,
                       lng_ref, lnb_ref, wco_ref, wao_ref, wo_ref, x1_ref):
    ob = _combine_groups([r[...].T for r in (la_ref, lb_ref, lc_ref)],
                         [r[...].T for r in (oa_ref, ob_ref, oc_ref)]).astype(BF16)
    x1_ref[...] = _merge(c_ref[...], x_ref[...], ob, gate_ref, lng_ref, lnb_ref, wco_ref, wao_ref, wo_ref)


def _sample_mix(c, x, ogs, lgs, gate, lng, lnb, wco, wao, wo):
    rows = x.shape[0]
    full = lambda n: _const_spec((rows, n))
    per_group = _const_spec((GROUP_WIDTH, rows))
    return pl.pallas_call(
        _sample_mix_kernel,
        grid=(1,),
        in_specs=[full(C_CONV), full(D_MODEL)] + [per_group] * (2 * N_GROUPS) + [full(2 * D_MODEL),
                  _const_spec((1, C_CONV)), _const_spec((1, C_CONV)),
                  _const_spec((C_CONV, D_MODEL)), _const_spec((GROUP_WIDTH, D_MODEL)),
                  _const_spec((D_MODEL, D_MODEL))],
        out_specs=pl.BlockSpec((rows, D_MODEL), lambda i: (0, 0)),
        out_shape=jax.ShapeDtypeStruct((rows, D_MODEL), F32),
        compiler_params=_params(1),
        name="sample_mix",
    )(c, x, *ogs, *lgs, gate, lng, lnb, wco, wao, wo)


FF_CHUNK = 256


def _ffn_chunks(hb, wfi_ref, wfo_ref, acc_ref, lo, hi):
    for c in range(lo * FF_CHUNK, hi * FF_CHUNK, FF_CHUNK):
        gch = jnp.dot(hb, wfi_ref[:, c:c + FF_CHUNK], preferred_element_type=F32)
        uch = jnp.dot(hb, wfi_ref[:, D_FF + c:D_FF + c + FF_CHUNK], preferred_element_type=F32)
        act = (jax.nn.silu(gch) * uch).astype(BF16)
        acc_ref[...] += jnp.dot(act, wfo_ref[c:c + FF_CHUNK, :], preferred_element_type=F32)


def _ffn_finish(x1, ffn_out, p_ref, gp_ref, wpg_ref, wpp_ref, gfin_ref, y_ref):
    x2 = x1 + ffn_out
    hp = _rms(x2, gp_ref[...]).astype(BF16)
    gate = jax.nn.sigmoid(jnp.dot(hp, wpg_ref[...], preferred_element_type=F32))
    pe = jnp.dot(p_ref[...].astype(BF16), wpp_ref[...], preferred_element_type=F32)
    x3 = x2 + gate * pe
    y_ref[...] = _rms(x3, gfin_ref[...])


def _ffn_kernel(x_ref, p_ref, gf_ref, wfi_ref, wfo_ref, gp_ref, wpg_ref, wpp_ref, gfin_ref, y_ref, acc_ref):
    x1 = x_ref[...]
    hb = _rms(x1, gf_ref[...]).astype(BF16)
    acc_ref[...] = jnp.zeros_like(acc_ref)
    _ffn_chunks(hb, wfi_ref, wfo_ref, acc_ref, 0, D_FF // FF_CHUNK)
    _ffn_finish(x1, acc_ref[...], p_ref, gp_ref, wpg_ref, wpp_ref, gfin_ref, y_ref)


def _ffn(x1, p, gf, wfi, wfo, gp, wpg, wpp, gfin, tm):
    rows = x1.shape[0]
    row = lambda n: pl.BlockSpec((tm, n), lambda i: (i, 0))
    return pl.pallas_call(
        _ffn_kernel,
        grid=(rows // tm,),
        in_specs=[row(D_MODEL), row(PLE_DIM), _const_spec((1, D_MODEL)),
                  _const_spec((D_MODEL, 2 * D_FF)), _const_spec((D_FF, D_MODEL)),
                  _const_spec((1, D_MODEL)), _const_spec((D_MODEL, D_MODEL)),
                  _const_spec((PLE_DIM, D_MODEL)), _const_spec((1, D_MODEL))],
        out_specs=row(D_MODEL),
        out_shape=jax.ShapeDtypeStruct((rows, D_MODEL), F32),
        scratch_shapes=[pltpu.VMEM((tm, D_MODEL), F32)],
        compiler_params=_params(1),
        name="ffn_ple",
    )(x1, p, gf, wfi, wfo, gp, wpg, wpp, gfin)


SAMPLE_CONV_BLOCK = 32


def _sample_conv_kernel(state_ref, u_ref, wdw_ref, bdw_ref, c_ref, new_ref):
    ctx = CONV_WIDTH - 1
    for lc in range(N_SLABS):
        cols = slice(lc * LANES, (lc + 1) * LANES)
        u = u_ref[lc]
        acc = u * wdw_ref[ctx:ctx + 1, cols] + bdw_ref[:, cols]
        for k in range(ctx):
            acc = acc + state_ref[k, :, cols] * wdw_ref[k:k + 1, cols]
        c_ref[:, cols] = acc
        new_ref[ctx - 1, :, cols] = u
    for k in range(ctx - 1):
        new_ref[k] = state_ref[k + 1]


def _sample_conv(state_t, u_slabs, wdw, bdw):
    ctx, n, _ = state_t.shape
    sb = SAMPLE_CONV_BLOCK
    return pl.pallas_call(
        _sample_conv_kernel,
        grid=(n // sb,),
        in_specs=[pl.BlockSpec((ctx, sb, C_CONV), lambda i: (0, i, 0)),
                  pl.BlockSpec((N_SLABS, sb, LANES), lambda i: (0, i, 0)),
                  _const_spec((CONV_WIDTH, C_CONV)), _const_spec((1, C_CONV))],
        out_specs=[pl.BlockSpec((sb, C_CONV), lambda i: (i, 0)),
                   pl.BlockSpec((ctx, sb, C_CONV), lambda i: (0, i, 0))],
        out_shape=[jax.ShapeDtypeStruct((n, C_CONV), F32),
                   jax.ShapeDtypeStruct((ctx, n, C_CONV), F32)],
        compiler_params=_params(1),
        name="sample_conv",
    )(state_t, u_slabs, wdw, bdw)


def _cache_sample(q_ref, new_ref, cache_ref, out_ref, o_ref, lse_ref, s, sample, dil):
    length = cache_ref.shape[-1]
    pos = lax.broadcasted_iota(jnp.int32, (1, length), 1)
    used = (pos & (dil - 1)) == 0
    lane = lax.broadcasted_iota(jnp.int32, (HEAD_DIM, LANES), 1)
    last = lane == LANES - 1
    mine = lane == sample

    def pick(ref, r0):
        return jnp.sum(jnp.where(mine, ref[r0:r0 + HEAD_DIM, :], 0.0), axis=1, keepdims=True)

    def shift(kv, h, new):
        rolled = pltpu.roll(cache_ref[s, kv, h], length - 1, 1)
        out_ref[s, kv, h] = rolled
        out_ref[s, kv, h, :, length - LANES:] = jnp.where(last, new, rolled[:, length - LANES:])

    heads = range(H_G)
    q = [pick(q_ref, h * HEAD_DIM) for h in heads]
    k_new = [pick(new_ref, h * HEAD_DIM) for h in heads]
    v_new = [pick(new_ref, GROUP_WIDTH + h * HEAD_DIM) for h in heads]
    shift(0, 0, k_new[0])
    shift(1, 0, v_new[0])
    sc = [jnp.where(used, jnp.sum(cache_ref[s, 0, h] * q[h], axis=0, keepdims=True), MASKED) for h in heads]
    s_new = [jnp.sum(k_new[h] * q[h], axis=0, keepdims=True) for h in heads]
    m = [jnp.maximum(jnp.max(sc[h], axis=1, keepdims=True), s_new[h]) for h in heads]
    shift(0, 1, k_new[1])
    shift(1, 1, v_new[1])
    p = [jnp.exp(sc[h] - m[h]) for h in heads]
    p_new = [jnp.exp(s_new[h] - m[h]) for h in heads]
    l = [jnp.sum(p[h], axis=1, keepdims=True) + p_new[h] for h in heads]
    shift(0, 2, k_new[2])
    shift(1, 2, v_new[2])
    o = [(jnp.sum(cache_ref[s, 1, h] * p[h], axis=1, keepdims=True) + v_new[h] * p_new[h]) / l[h]
         for h in heads]
    shift(0, 3, k_new[3])
    shift(1, 3, v_new[3])
    for h in heads:
        rows = slice(h * HEAD_DIM, (h + 1) * HEAD_DIM)
        o_ref[rows, :] = jnp.where(mine, o[h], o_ref[rows, :])
        lse_ref[rows, :] = jnp.where(mine, m[h] + jnp.log(l[h]), lse_ref[rows, :])


def _sample_cache_kernel(q_ref, new_ref, cache_ref, out_ref, o_ref, lse_ref, *, dil):
    sb = cache_ref.shape[0]
    step = pl.program_id(0)

    @pl.when(step == 0)
    def _():
        o_ref[...] = jnp.zeros_like(o_ref)
        lse_ref[...] = jnp.zeros_like(lse_ref)

    def sample(s, carry):
        _cache_sample(q_ref, new_ref, cache_ref, out_ref, o_ref, lse_ref, s, step * sb + s, dil)
        return carry

    lax.fori_loop(0, sb, sample, 0)


SAMPLE_CACHE_BLOCK_BYTES = 4 * 1024 * 1024


def _sample_cache(q_t, new_t, cache_t, g):
    n, _, _, _, length = cache_t.shape
    sb = max(1, SAMPLE_CACHE_BLOCK_BYTES // (2 * H_G * HEAD_DIM * length * 4))
    big = pl.BlockSpec((sb, 2, H_G, HEAD_DIM, length), lambda i: (i, 0, 0, 0, 0))
    res = pl.BlockSpec((GROUP_WIDTH, n), lambda i: (0, 0))
    return pl.pallas_call(
        functools.partial(_sample_cache_kernel, dil=GROUPS[g][1]),
        grid=(n // sb,),
        in_specs=[pl.BlockSpec((GROUP_WIDTH, n), lambda i, g=g: (g, 0)),
                  pl.BlockSpec((2 * GROUP_WIDTH, n), lambda i: (0, 0)), big],
        out_specs=[big, res, res],
        out_shape=[jax.ShapeDtypeStruct(cache_t.shape, F32),
                   jax.ShapeDtypeStruct((GROUP_WIDTH, n), F32),
                   jax.ShapeDtypeStruct((GROUP_WIDTH, n), F32)],
        compiler_params=_params(1),
        name="sample_cache_" + "abc"[g],
    )(q_t, new_t, cache_t)


FFN_PARTS = 4


def _ffn_cache_kernel(x_ref, p_ref, gf_ref, wfi_ref, wfo_ref, gp_ref, wpg_ref, wpp_ref, gfin_ref,
                      q_ref, new_ref, cache_ref, y_ref, out_ref, o_ref, lse_ref, acc_ref, hb_ref, *, dil):
    tile = pl.program_id(0)
    part = pl.program_id(1)

    @pl.when((tile == 0) & (part == 0))
    def _():
        o_ref[...] = jnp.zeros_like(o_ref)
        lse_ref[...] = jnp.zeros_like(lse_ref)

    n_chunks = D_FF // FF_CHUNK
    bounds = [(n_chunks + 2) * j // FFN_PARTS for j in range(FFN_PARTS)] + [n_chunks]

    for j in range(FFN_PARTS):
        @pl.when(part == j)
        def _(j=j):
            if j == 0:
                hb_ref[...] = _rms(x_ref[...], gf_ref[...]).astype(BF16)
                acc_ref[...] = jnp.zeros_like(acc_ref)
            _ffn_chunks(hb_ref[...], wfi_ref, wfo_ref, acc_ref, bounds[j], bounds[j + 1])
            if j == FFN_PARTS - 1:
                _ffn_finish(x_ref[...], acc_ref[...], p_ref, gp_ref, wpg_ref, wpp_ref, gfin_ref, y_ref)
            _cache_sample(q_ref, new_ref, cache_ref, out_ref, o_ref, lse_ref, 0, tile * FFN_PARTS + part, dil)


def _ffn_cache(x1, p, gf, wfi, wfo, gp, wpg, wpp, gfin, q_t, new_t, cache_t, g):
    tm = PROMPT_TM
    rows = x1.shape[0]
    n, _, _, _, length = cache_t.shape
    assert (rows // tm) * FFN_PARTS == n
    row = lambda w: pl.BlockSpec((tm, w), lambda i, k: (i, 0))
    big = pl.BlockSpec((1, 2, H_G, HEAD_DIM, length), lambda i, k: (i * FFN_PARTS + k, 0, 0, 0, 0))
    res = pl.BlockSpec((GROUP_WIDTH, n), lambda i, k: (0, 0))
    return pl.pallas_call(
        functools.partial(_ffn_cache_kernel, dil=GROUPS[g][1]),
        grid=(rows // tm, FFN_PARTS),
        in_specs=[row(D_MODEL), row(PLE_DIM), _const_spec((1, D_MODEL)),
                  _const_spec((D_MODEL, 2 * D_FF)), _const_spec((D_FF, D_MODEL)),
                  _const_spec((1, D_MODEL)), _const_spec((D_MODEL, D_MODEL)),
                  _const_spec((PLE_DIM, D_MODEL)), _const_spec((1, D_MODEL)),
                  pl.BlockSpec((GROUP_WIDTH, n), lambda i, k, g=g: (g, 0)),
                  pl.BlockSpec((2 * GROUP_WIDTH, n), lambda i, k: (0, 0)), big],
        out_specs=[row(D_MODEL), big, res, res],
        out_shape=[jax.ShapeDtypeStruct((rows, D_MODEL), F32),
                   jax.ShapeDtypeStruct(cache_t.shape, F32),
                   jax.ShapeDtypeStruct((GROUP_WIDTH, n), F32),
                   jax.ShapeDtypeStruct((GROUP_WIDTH, n), F32)],
        scratch_shapes=[pltpu.VMEM((tm, D_MODEL), F32), pltpu.VMEM((tm, D_MODEL), BF16)],
        compiler_params=_params(2),
        name="ffn_ple_cache_" + "abc"[g],
    )(x1, p, gf, wfi, wfo, gp, wpg, wpp, gfin, q_t, new_t, cache_t)


def _rope_tables(pos):
    half = ROT_DIM // 2
    inv_freq = ROPE_THETA ** (-jnp.arange(half, dtype=F32) / half)
    ang = pos.astype(F32)[:, None] * inv_freq[None, :]
    cos, sin = jnp.cos(ang), jnp.sin(ang)
    n = pos.shape[0]
    zeros = lambda w: jnp.zeros((n, w), F32)
    cos_h = jnp.concatenate([cos, cos, jnp.ones((n, HEAD_DIM - ROT_DIM), F32)], axis=-1)
    sa_h = jnp.concatenate([-sin, zeros(HEAD_DIM - half)], axis=-1)
    sb_h = jnp.concatenate([zeros(half), sin, zeros(HEAD_DIM - ROT_DIM)], axis=-1)
    return tuple(jnp.tile(t, (1, H_G)) for t in (cos_h, sa_h, sb_h))


def _position_major(t):
    b, _, length = t.shape
    return jnp.transpose(t.reshape(b, 2, H_G, HEAD_DIM, length), (0, 4, 1, 2, 3))[None]


def kernel(x_prompt, x_sample, state_conv, cache_win_a, cache_win_b, cache_win_c, p_prompt, p_sample,
           w_in, g_mix, w_dw, b_dw, ln_g, ln_b, w_conv_out, w_attn_out, w_o, g_ffn, w_ffn_in, w_ffn_out,
           g_ple, w_ple_gate, w_ple_proj, g_final):
    assert w_in.shape[0] == 1, "single-layer step"
    w_in_b = w_in[0].astype(BF16)
    wco = w_conv_out[0].astype(BF16)
    wao = w_attn_out[0].astype(BF16)
    wo = w_o[0].astype(BF16)
    wfi = w_ffn_in[0].astype(BF16)
    wfo = w_ffn_out[0].astype(BF16)
    wpg = w_ple_gate[0].astype(BF16)
    wpp = w_ple_proj[0].astype(BF16)
    gfin = g_final.reshape(1, D_MODEL)
    n_prompt = BATCH * SEQ
    ctx = CONV_WIDTH - 1

    xp = x_prompt.reshape(n_prompt, D_MODEL)
    tabs_p = _rope_tables(jnp.arange(SEQ, dtype=jnp.int32))
    c_slabs, q, kva, kvb, kvc, gates, tail_a, tail_b, tail_c, u_tail = _in_proj_prompt(
        xp, g_mix, w_in_b, *tabs_p, w_dw[0], b_dw)
    per_batch = lambda t: t.reshape(BATCH, SEQ, t.shape[-1])
    o = _attention(per_batch(q), per_batch(kva), per_batch(kvb), per_batch(kvc))
    x1 = _prompt_mix(c_slabs, xp, o.reshape(n_prompt, GROUP_WIDTH), gates, ln_g, ln_b, wco, wao, wo)
    new_conv_prompt = u_tail[None, :, CONV_HALO - ctx:, :]

    n_s = DEC_BATCH
    xs = x_sample.reshape(n_s, D_MODEL)
    tabs_s = tuple(jnp.broadcast_to(t, (n_s, GROUP_WIDTH))
                   for t in _rope_tables(jnp.full((1,), PAST_LEN, jnp.int32)))
    u_s, q_t, new_a, new_b, new_c, gates_s = _in_proj_sample(xs, g_mix, w_in_b, *tabs_s)
    c_s, new_conv_t = _sample_conv(jnp.transpose(state_conv[0], (1, 0, 2)), u_s, w_dw[0], b_dw)
    new_wins, ogs, lgs = [], [], []
    for g, (new_t, cache) in enumerate(zip((new_a, new_b, new_c), (cache_win_a, cache_win_b, cache_win_c))):
        cache_t = jnp.transpose(cache[0], (0, 2, 3, 4, 1))
        if g < N_GROUPS - 1:
            shifted, o_g, lse_g = _sample_cache(q_t, new_t, cache_t, g)
        else:
            y_prompt, shifted, o_g, lse_g = _ffn_cache(
                x1, p_prompt.reshape(n_prompt, PLE_DIM), g_ffn, wfi, wfo, g_ple, wpg, wpp, gfin,
                q_t, new_t, cache_t, g)
        new_wins.append(jnp.transpose(shifted, (0, 4, 1, 2, 3))[None])
        ogs.append(o_g)
        lgs.append(lse_g)
    x1_s = _sample_mix(c_s, xs, ogs, lgs, gates_s, ln_g, ln_b, wco, wao, wo)
    y_sample = _ffn(x1_s, p_sample.reshape(n_s, PLE_DIM), g_ffn, wfi, wfo, g_ple, wpg, wpp, gfin,
                    n_s).reshape(n_s, 1, D_MODEL)
    new_conv_sample = jnp.transpose(new_conv_t, (1, 0, 2))[None]

    return (y_prompt.reshape(BATCH, SEQ, D_MODEL), y_sample, new_conv_prompt, _position_major(tail_a), _position_major(tail_b),
            _position_major(tail_c), new_conv_sample, new_wins[0], new_wins[1], new_wins[2])
```

```python
import jax
import jax.numpy as jnp
from jax import lax
from jax.experimental import pallas as pl
from jax.experimental.pallas import tpu as pltpu

D_MODEL = 1024
BATCH = 8
SEQ = 2048
DEC_BATCH = 128
PAST_LEN = 8192
HEAD_DIM = 64
GROUPS = ((128, 1), (512, 4), (2048, 16))
H_G = 4
N_GROUPS = len(GROUPS)
GROUP_WIDTH = H_G * HEAD_DIM
ATTN_WIDTH = N_GROUPS * GROUP_WIDTH
ROT_DIM = HEAD_DIM // 4
ROPE_THETA = 500000.0
C_CONV = D_MODEL
CONV_WIDTH = 31
D_FF = 2816
PLE_DIM = 256
NORM_EPS = 1e-6
NK = 128
Q_OFF = 2 * C_CONV
K_OFF = Q_OFF + ATTN_WIDTH
V_OFF = K_OFF + ATTN_WIDTH
GATE_OFF = V_OFF + ATTN_WIDTH
IN_COLS = GATE_OFF + 2 * D_MODEL

F32 = jnp.float32
BF16 = jnp.bfloat16
LANES = 128
SUBLANES = 8
N_SLABS = C_CONV // LANES
MASKED = -1e30
VMEM_LIMIT = 52 * 1024 * 1024


def _const_spec(shape):
    return pl.BlockSpec(shape, lambda *_: (0,) * len(shape), pipeline_mode=pl.Buffered(1))


def _params(n_axes):
    return pltpu.CompilerParams(dimension_semantics=("arbitrary",) * n_axes,
                                vmem_limit_bytes=VMEM_LIMIT)


def _rms(x, g):
    return x * lax.rsqrt(jnp.mean(x * x, axis=-1, keepdims=True) + NORM_EPS) * g


PROMPT_TM = 512
TILES_PER_SEQ = SEQ // PROMPT_TM
CONV_HALO = 32
CONV_PITCH = 4
CONV_GROUP = CONV_PITCH * SUBLANES
CONV_UNROLL = 1


def _projector(x_ref, g_ref, w_ref, cos_ref, sa_ref, sb_ref):
    hb = _rms(x_ref[...], g_ref[...]).astype(BF16)
    ch = GROUP_WIDTH

    def mm(c0):
        return jnp.dot(hb, w_ref[:, c0:c0 + ch], preferred_element_type=F32)

    def glu(c):
        return mm(c) * jax.nn.sigmoid(mm(C_CONV + c))

    def gates(gate_ref, lo=0, hi=2 * D_MODEL // ch):
        for c in range(lo * ch, hi * ch, ch):
            gate_ref[:, c:c + ch] = jax.nn.sigmoid(mm(GATE_OFF + c)).astype(BF16)

    def rope(z):
        return (z * cos_ref[...] + pltpu.roll(z, ch - ROT_DIM // 2, 1) * sa_ref[...]
                + pltpu.roll(z, ROT_DIM // 2, 1) * sb_ref[...])

    def qkv(g):
        c = g * ch
        return rope(mm(Q_OFF + c)) * (HEAD_DIM ** -0.5), rope(mm(K_OFF + c)), mm(V_OFF + c)

    return qkv, glu, gates


def _conv_slab(win_ref, lc, wdw_ref, bdw_ref, c_ref, n_rows):
    first = CONV_HALO - (CONV_WIDTH - 1)
    cols = slice(lc * LANES, (lc + 1) * LANES)
    n_acc = CONV_UNROLL * CONV_PITCH
    offs = [(i // CONV_PITCH) * CONV_GROUP + i % CONV_PITCH for i in range(n_acc)]
    bias = jnp.broadcast_to(bdw_ref[:, cols], (SUBLANES, LANES))
    for base in range(0, n_rows, CONV_UNROLL * CONV_GROUP):
        accs = [bias] * n_acc
        for k in range(CONV_WIDTH):
            wk = wdw_ref[k:k + 1, cols]
            for i in range(n_acc):
                tap = win_ref[lc, pl.ds(base + first + k + offs[i], SUBLANES, stride=CONV_PITCH), :]
                accs[i] = accs[i] + tap * wk
        for i in range(n_acc):
            c_ref[lc, pl.ds(base + offs[i], SUBLANES, stride=CONV_PITCH), :] = accs[i]


def _in_proj_prompt_kernel(x_ref, g_ref, w_ref, cos_ref, sa_ref, sb_ref, wdw_ref, bdw_ref,
                           c_ref, q_ref, kva_ref, kvb_ref, kvc_ref, gate_ref, ta_ref, tb_ref, tc_ref, ut_ref,
                           win_ref):
    tm = PROMPT_TM
    ch = GROUP_WIDTH

    @pl.when(pl.program_id(0) == 0)
    def _():
        win_ref[:, tm:tm + CONV_HALO, :] = jnp.zeros((N_SLABS, CONV_HALO, LANES), F32)

    qkv, glu, gates = _projector(x_ref, g_ref, w_ref, cos_ref, sa_ref, sb_ref)
    first_tile = pl.program_id(0) % TILES_PER_SEQ == 0
    def head_group(g):
        kv_ref, t_ref = ((kva_ref, ta_ref), (kvb_ref, tb_ref), (kvc_ref, tc_ref))[g]
        q, k, v = qkv(g)
        q_ref[:, g * ch:(g + 1) * ch] = q
        kv_ref[:, 0:ch] = k
        kv_ref[:, ch:2 * ch] = v
        n = t_ref.shape[1]
        t_ref[0:ch, :] = k[tm - n:, :].T
        t_ref[ch:2 * ch, :] = v[tm - n:, :].T

    def glu_chunk(i):
        c = i * ch
        u = glu(c)
        ut_ref[:, c:c + ch] = u[tm - CONV_HALO:, :]
        for j in range(ch // LANES):
            lc = c // LANES + j
            win_ref[lc, 0:CONV_HALO, :] = jnp.where(first_tile, 0.0, win_ref[lc, tm:tm + CONV_HALO, :])
            win_ref[lc, CONV_HALO:CONV_HALO + tm, :] = u[:, j * LANES:(j + 1) * LANES]

    n_gate = 2 * D_MODEL // ch
    after_slab = [
        lambda: glu_chunk(1), lambda: head_group(0), lambda: glu_chunk(2), lambda: head_group(1),
        lambda: glu_chunk(3), lambda: head_group(2),
        lambda: gates(gate_ref, 0, n_gate // 2), lambda: gates(gate_ref, n_gate // 2, n_gate)]
    glu_chunk(0)
    for lc in range(N_SLABS):
        _conv_slab(win_ref, lc, wdw_ref, bdw_ref, c_ref, tm)
        after_slab[lc]()


def _in_proj_prompt(x2d, g, w, cos, sa, sb, wdw, bdw):
    tm = PROMPT_TM
    rows = x2d.shape[0]
    assert GROUPS[1][0] == tm and GROUPS[2][0] == SEQ and GROUPS[0][0] <= tm
    row = lambda n: pl.BlockSpec((tm, n), lambda i: (i, 0))
    tab = pl.BlockSpec((tm, GROUP_WIDTH), lambda i: (i % TILES_PER_SEQ, 0))
    kv_w = 2 * GROUP_WIDTH
    tail = lambda n: pl.BlockSpec((None, kv_w, n), lambda i: (i // TILES_PER_SEQ, 0, 0))
    return pl.pallas_call(
        _in_proj_prompt_kernel,
        grid=(rows // tm,),
        in_specs=[row(D_MODEL), _const_spec((1, D_MODEL)), _const_spec((D_MODEL, IN_COLS)), tab, tab, tab,
                  _const_spec((CONV_WIDTH, C_CONV)), _const_spec((1, C_CONV))],
        out_specs=[pl.BlockSpec((N_SLABS, tm, LANES), lambda i: (0, i, 0)),
                   row(ATTN_WIDTH), row(kv_w), row(kv_w), row(kv_w), row(2 * D_MODEL),
                   tail(GROUPS[0][0]), tail(GROUPS[1][0]),
                   pl.BlockSpec((None, kv_w, tm), lambda i: (i // TILES_PER_SEQ, 0, i % TILES_PER_SEQ)),
                   pl.BlockSpec((None, CONV_HALO, C_CONV), lambda i: (i // TILES_PER_SEQ, 0, 0))],
        out_shape=[jax.ShapeDtypeStruct((N_SLABS, rows, LANES), F32),
                   jax.ShapeDtypeStruct((rows, ATTN_WIDTH), F32),
                   jax.ShapeDtypeStruct((rows, kv_w), F32),
                   jax.ShapeDtypeStruct((rows, kv_w), F32),
                   jax.ShapeDtypeStruct((rows, kv_w), F32),
                   jax.ShapeDtypeStruct((rows, 2 * D_MODEL), BF16),
                   jax.ShapeDtypeStruct((BATCH, kv_w, GROUPS[0][0]), F32),
                   jax.ShapeDtypeStruct((BATCH, kv_w, GROUPS[1][0]), F32),
                   jax.ShapeDtypeStruct((BATCH, kv_w, SEQ), F32),
                   jax.ShapeDtypeStruct((BATCH, CONV_HALO, C_CONV), F32)],
        scratch_shapes=[pltpu.VMEM((N_SLABS, CONV_HALO + tm, LANES), F32)],
        compiler_params=_params(1),
        name="in_proj_prompt",
    )(x2d, g, w, cos, sa, sb, wdw, bdw)


def _in_proj_sample_kernel(x_ref, g_ref, w_ref, cos_ref, sa_ref, sb_ref,
                           u_ref, qt_ref, ta_ref, tb_ref, tc_ref, gate_ref):
    qkv, glu, gates = _projector(x_ref, g_ref, w_ref, cos_ref, sa_ref, sb_ref)
    ch = GROUP_WIDTH
    for g, t_ref in enumerate((ta_ref, tb_ref, tc_ref)):
        q, k, v = qkv(g)
        qt_ref[g * ch:(g + 1) * ch, :] = q.T
        t_ref[0:ch, :] = k.T
        t_ref[ch:2 * ch, :] = v.T
    for c in range(0, C_CONV, ch):
        u = glu(c)
        for j in range(ch // LANES):
            u_ref[c // LANES + j] = u[:, j * LANES:(j + 1) * LANES]
    gates(gate_ref)


def _in_proj_sample(x2d, g, w, cos, sa, sb):
    n = x2d.shape[0]
    full = lambda r, c: pl.BlockSpec((r, c), lambda i: (0, 0))
    kv_w = 2 * GROUP_WIDTH
    return pl.pallas_call(
        _in_proj_sample_kernel,
        grid=(1,),
        in_specs=[full(n, D_MODEL), _const_spec((1, D_MODEL)), _const_spec((D_MODEL, IN_COLS)),
                  full(n, GROUP_WIDTH), full(n, GROUP_WIDTH), full(n, GROUP_WIDTH)],
        out_specs=[pl.BlockSpec((N_SLABS, n, LANES), lambda i: (0, 0, 0)),
                   full(ATTN_WIDTH, n), full(kv_w, n), full(kv_w, n), full(kv_w, n), full(n, 2 * D_MODEL)],
        out_shape=[jax.ShapeDtypeStruct((N_SLABS, n, LANES), F32),
                   jax.ShapeDtypeStruct((ATTN_WIDTH, n), F32),
                   jax.ShapeDtypeStruct((kv_w, n), F32),
                   jax.ShapeDtypeStruct((kv_w, n), F32),
                   jax.ShapeDtypeStruct((kv_w, n), F32),
                   jax.ShapeDtypeStruct((n, 2 * D_MODEL), BF16)],
        compiler_params=_params(1),
        name="in_proj_sample",
    )(x2d, g, w, cos, sa, sb)


ATTN_UNROLL = 4


def _combine_groups(lses, outs):
    mx = jnp.maximum(jnp.maximum(lses[0], lses[1]), lses[2])
    es = [jnp.exp(l - mx) for l in lses]
    num = es[0] * outs[0] + es[1] * outs[1] + es[2] * outs[2]
    return num / (es[0] + es[1] + es[2])


def _largest_divisor(n, cap):
    return max(d for d in range(1, cap + 1) if n % d == 0)


def _attn_kernel(qa_ref, qb_ref, qc_ref, ka_ref, va_ref, kb_ref, vb_ref, kc_ref, vc_ref,
                 o_ref, og_ref, lg_ref):
    lane = lax.broadcasted_iota(jnp.int32, (NK, LANES), 1)
    qi = lax.broadcasted_iota(jnp.int32, (NK, LANES), 0)
    first_head = lane < HEAD_DIM
    own_mask = lane <= qi
    prev_mask = lane >= qi
    nt = (((1,), (1,)), ((), ()))

    def blocks(g, q_ref, k_ref, v_ref, row_list):
        scores, values = [], []
        for rows, prev_rows in row_list:
            qb = q_ref[rows, :]
            keys = [k_ref[rows, :].astype(BF16)]
            vals = [v_ref[rows, :].astype(BF16)]
            masks = [own_mask]
            if prev_rows is not None:
                keys.append(k_ref[prev_rows, :].astype(BF16))
                vals.append(v_ref[prev_rows, :].astype(BF16))
                masks.append(prev_mask)
            values.append(jnp.concatenate(vals, axis=0) if len(vals) > 1 else vals[0])
            for head_mask in (first_head, jnp.logical_not(first_head)):
                qh = jnp.where(head_mask, qb, 0.0).astype(BF16)
                scores.append([jnp.where(mk, lax.dot_general(qh, kk, nt, preferred_element_type=F32), MASKED)
                               for kk, mk in zip(keys, masks)])
        probs, maxes = [], []
        for parts in scores:
            top = parts[0] if len(parts) == 1 else jnp.maximum(parts[0], parts[1])
            m = jnp.max(top, axis=-1, keepdims=True)
            es = [jnp.exp(s - m) for s in parts]
            probs.append((jnp.concatenate(es, axis=1) if len(es) > 1 else es[0]).astype(BF16))
            maxes.append(m)
        for b, (rows, _) in enumerate(row_list):
            vals = values[b]
            ones = jnp.ones((vals.shape[0], LANES), BF16)
            outs, lses = [], []
            for h in range(2):
                p = probs[2 * b + h]
                l = jnp.dot(p, ones, preferred_element_type=F32)
                outs.append(jnp.dot(p, vals, preferred_element_type=F32) / l)
                lses.append(maxes[2 * b + h] + jnp.log(l))
            og_ref[g, rows, :] = jnp.where(first_head, outs[0], outs[1])
            lg_ref[g, rows, :] = jnp.where(first_head, lses[0], lses[1])

    for g, (q_ref, k_ref, v_ref) in enumerate(((qa_ref, ka_ref, va_ref), (qb_ref, kb_ref, vb_ref),
                                               (qc_ref, kc_ref, vc_ref))):
        dil = GROUPS[g][1]
        span = NK * dil
        n_later = dil * (SEQ // span - 1)

        def rows_at(start, dil=dil):
            return pl.ds(start, NK) if dil == 1 else pl.ds(start, NK, stride=dil)

        n_first = _largest_divisor(dil, ATTN_UNROLL)

        def first(t, carry, g=g, q_ref=q_ref, k_ref=k_ref, v_ref=v_ref, rows_at=rows_at, n=n_first):
            blocks(g, q_ref, k_ref, v_ref, [(rows_at(t * n + j), None) for j in range(n)])
            return carry

        lax.fori_loop(0, dil // n_first, first, 0)
        if n_later:
            n_per = _largest_divisor(n_later, ATTN_UNROLL)

            def later(t, carry, g=g, q_ref=q_ref, k_ref=k_ref, v_ref=v_ref, rows_at=rows_at, dil=dil,
                      span=span, n=n_per):
                row_list = []
                for j in range(n):
                    i = t * n + j
                    start = i % dil + (1 + i // dil) * span
                    row_list.append((rows_at(start), rows_at(start - span)))
                blocks(g, q_ref, k_ref, v_ref, row_list)
                return carry

            lax.fori_loop(0, n_later // n_per, later, 0)

    tc = 256
    for t in range(0, SEQ, tc):
        rows = slice(t, t + tc)
        o_ref[rows, :] = _combine_groups([lg_ref[g, rows, :] for g in range(N_GROUPS)],
                                         [og_ref[g, rows, :] for g in range(N_GROUPS)]).astype(o_ref.dtype)


def _attention(q3, kva3, kvb3, kvc3):
    def col(c):
        return pl.BlockSpec((None, SEQ, LANES), lambda b, hp, c=c: (b, 0, c + hp))
    n_pairs = GROUP_WIDTH // LANES
    return pl.pallas_call(
        _attn_kernel,
        grid=(BATCH, n_pairs),
        in_specs=[col(0), col(n_pairs), col(2 * n_pairs),
                  col(0), col(n_pairs), col(0), col(n_pairs), col(0), col(n_pairs)],
        out_specs=pl.BlockSpec((None, SEQ, LANES), lambda b, hp: (b, 0, hp)),
        out_shape=jax.ShapeDtypeStruct((BATCH, SEQ, GROUP_WIDTH), BF16),
        scratch_shapes=[pltpu.VMEM((N_GROUPS, SEQ, LANES), F32), pltpu.VMEM((N_GROUPS, SEQ, LANES), F32)],
        compiler_params=_params(2),
        name="prompt_attention",
    )(q3, q3, q3, kva3, kva3, kvb3, kvb3, kvc3, kvc3)


def _merge(c, x, ob, gate_ref, lng_ref, lnb_ref, wco_ref, wao_ref, wo_ref):
    mu = jnp.mean(c, axis=-1, keepdims=True)
    d = c - mu
    var = jnp.mean(d * d, axis=-1, keepdims=True)
    y = d * lax.rsqrt(var + NORM_EPS) * lng_ref[...] + lnb_ref[...]
    a_out = jnp.dot(jax.nn.silu(y).astype(BF16), wco_ref[...], preferred_element_type=F32)
    b_out = jnp.dot(ob, wao_ref[...], preferred_element_type=F32)
    merged = (gate_ref[:, 0:D_MODEL].astype(F32) * a_out
              + gate_ref[:, D_MODEL:2 * D_MODEL].astype(F32) * b_out)
    return x + jnp.dot(merged.astype(BF16), wo_ref[...], preferred_element_type=F32)


MIX_TM = 512


def _prompt_mix_kernel(c_ref, x_ref, o_ref, gate_ref, lng_ref, lnb_ref, wco_ref, wao_ref, wo_ref, x1_ref):
    c = jnp.concatenate([c_ref[lc] for lc in range(N_SLABS)], axis=-1)
    x1_ref[...] = _merge(c, x_ref[...], o_ref[...], gate_ref, lng_ref, lnb_ref, wco_ref, wao_ref, wo_ref)


def _prompt_mix(c_slabs, x2d, o2d, gate2d, lng, lnb, wco, wao, wo):
    tm = MIX_TM
    rows = x2d.shape[0]
    row = lambda n: pl.BlockSpec((tm, n), lambda i: (i, 0))
    return pl.pallas_call(
        _prompt_mix_kernel,
        grid=(rows // tm,),
        in_specs=[pl.BlockSpec((N_SLABS, tm, LANES), lambda i: (0, i, 0)),
                  row(D_MODEL), row(GROUP_WIDTH), row(2 * D_MODEL),
                  _const_spec((1, C_CONV)), _const_spec((1, C_CONV)),
                  _const_spec((C_CONV, D_MODEL)), _const_spec((GROUP_WIDTH, D_MODEL)),
                  _const_spec((D_MODEL, D_MODEL))],
        out_specs=row(D_MODEL),
        out_shape=jax.ShapeDtypeStruct((rows, D_MODEL), F32),
        compiler_params=_params(1),
        name="prompt_mix",
    )(c_slabs, x2d, o2d, gate2d, lng, lnb, wco, wao, wo)


def _sample_mix_kernel(c_ref, x_ref, oa_ref, ob_ref, oc_ref, la_ref, lb_ref, lc_ref, gate_ref,
                       lng_ref, lnb_ref, wco_ref, wao_ref, wo_ref, x1_ref):
    ob = _combine_groups([r[...].T for r in (la_ref, lb_ref, lc_ref)],
                         [r[...].T for r in (oa_ref, ob_ref, oc_ref)]).astype(BF16)
    x1_ref[...] = _merge(c_ref[...], x_ref[...], ob, gate_ref, lng_ref, lnb_ref, wco_ref, wao_ref, wo_ref)


def _sample_mix(c, x, ogs, lgs, gate, lng, lnb, wco, wao, wo):
    rows = x.shape[0]
    full = lambda n: _const_spec((rows, n))
    per_group = _const_spec((GROUP_WIDTH, rows))
    return pl.pallas_call(
        _sample_mix_kernel,
        grid=(1,),
        in_specs=[full(C_CONV), full(D_MODEL)] + [per_group] * (2 * N_GROUPS) + [full(2 * D_MODEL),
                  _const_spec((1, C_CONV)), _const_spec((1, C_CONV)),
                  _const_spec((C_CONV, D_MODEL)), _const_spec((GROUP_WIDTH, D_MODEL)),
                  _const_spec((D_MODEL, D_MODEL))],
        out_specs=pl.BlockSpec((rows, D_MODEL), lambda i: (0, 0)),
        out_shape=jax.ShapeDtypeStruct((rows, D_MODEL), F32),
        compiler_params=_params(1),
        name="sample_mix",
    )(c, x, *ogs, *lgs, gate, lng, lnb, wco, wao, wo)


FF_CHUNK = 256


def _ffn_chunks(hb, wfi_ref, wfo_ref, acc_ref, lo, hi):
    for c in range(lo * FF_CHUNK, hi * FF_CHUNK, FF_CHUNK):
        gch = jnp.dot(hb, wfi_ref[:, c:c + FF_CHUNK], preferred_element_type=F32)
        uch = jnp.dot(hb, wfi_ref[:, D_FF + c:D_FF + c + FF_CHUNK], preferred_element_type=F32)
        act = (jax.nn.silu(gch) * uch).astype(BF16)
        acc_ref[...] += jnp.dot(act, wfo_ref[c:c + FF_CHUNK, :], preferred_element_type=F32)


def _ffn_finish(x1, ffn_out, p_ref, gp_ref, wpg_ref, wpp_ref, gfin_ref, y_ref):
    x2 = x1 + ffn_out
    hp = _rms(x2, gp_ref[...]).astype(BF16)
    gate = jax.nn.sigmoid(jnp.dot(hp, wpg_ref[...], preferred_element_type=F32))
    pe = jnp.dot(p_ref[...].astype(BF16), wpp_ref[...], preferred_element_type=F32)
    x3 = x2 + gate * pe
    y_ref[...] = _rms(x3, gfin_ref[...])


def _ffn_kernel(x_ref, p_ref, gf_ref, wfi_ref, wfo_ref, gp_ref, wpg_ref, wpp_ref, gfin_ref, y_ref, acc_ref):
    x1 = x_ref[...]
    hb = _rms(x1, gf_ref[...]).astype(BF16)
    acc_ref[...] = jnp.zeros_like(acc_ref)
    _ffn_chunks(hb, wfi_ref, wfo_ref, acc_ref, 0, D_FF // FF_CHUNK)
    _ffn_finish(x1, acc_ref[...], p_ref, gp_ref, wpg_ref, wpp_ref, gfin_ref, y_ref)


def _ffn(x1, p, gf, wfi, wfo, gp, wpg, wpp, gfin, tm):
    rows = x1.shape[0]
    row = lambda n: pl.BlockSpec((tm, n), lambda i: (i, 0))
    return pl.pallas_call(
        _ffn_kernel,
        grid=(rows // tm,),
        in_specs=[row(D_MODEL), row(PLE_DIM), _const_spec((1, D_MODEL)),
                  _const_spec((D_MODEL, 2 * D_FF)), _const_spec((D_FF, D_MODEL)),
                  _const_spec((1, D_MODEL)), _const_spec((D_MODEL, D_MODEL)),
                  _const_spec((PLE_DIM, D_MODEL)), _const_spec((1, D_MODEL))],
        out_specs=row(D_MODEL),
        out_shape=jax.ShapeDtypeStruct((rows, D_MODEL), F32),
        scratch_shapes=[pltpu.VMEM((tm, D_MODEL), F32)],
        compiler_params=_params(1),
        name="ffn_ple",
    )(x1, p, gf, wfi, wfo, gp, wpg, wpp, gfin)


SAMPLE_CONV_BLOCK = 32


def _sample_conv_kernel(state_ref, u_ref, wdw_ref, bdw_ref, c_ref, new_ref):
    ctx = CONV_WIDTH - 1
    for lc in range(N_SLABS):
        cols = slice(lc * LANES, (lc + 1) * LANES)
        u = u_ref[lc]
        acc = u * wdw_ref[ctx:ctx + 1, cols] + bdw_ref[:, cols]
        for k in range(ctx):
            acc = acc + state_ref[k, :, cols] * wdw_ref[k:k + 1, cols]
        c_ref[:, cols] = acc
        new_ref[ctx - 1, :, cols] = u
    for k in range(ctx - 1):
        new_ref[k] = state_ref[k + 1]


def _sample_conv(state_t, u_slabs, wdw, bdw):
    ctx, n, _ = state_t.shape
    sb = SAMPLE_CONV_BLOCK
    return pl.pallas_call(
        _sample_conv_kernel,
        grid=(n // sb,),
        in_specs=[pl.BlockSpec((ctx, sb, C_CONV), lambda i: (0, i, 0)),
                  pl.BlockSpec((N_SLABS, sb, LANES), lambda i: (0, i, 0)),
                  _const_spec((CONV_WIDTH, C_CONV)), _const_spec((1, C_CONV))],
        out_specs=[pl.BlockSpec((sb, C_CONV), lambda i: (i, 0)),
                   pl.BlockSpec((ctx, sb, C_CONV), lambda i: (0, i, 0))],
        out_shape=[jax.ShapeDtypeStruct((n, C_CONV), F32),
                   jax.ShapeDtypeStruct((ctx, n, C_CONV), F32)],
        compiler_params=_params(1),
        name="sample_conv",
    )(state_t, u_slabs, wdw, bdw)


def _cache_sample(q_ref, new_ref, cache_ref, out_ref, o_ref, lse_ref, s, sample, dil):
    length = cache_ref.shape[-1]
    pos = lax.broadcasted_iota(jnp.int32, (1, length), 1)
    used = (pos & (dil - 1)) == 0
    lane = lax.broadcasted_iota(jnp.int32, (HEAD_DIM, LANES), 1)
    last = lane == LANES - 1
    mine = lane == sample

    def pick(ref, r0):
        return jnp.sum(jnp.where(mine, ref[r0:r0 + HEAD_DIM, :], 0.0), axis=1, keepdims=True)

    def shift(kv, h, new):
        rolled = pltpu.roll(cache_ref[s, kv, h], length - 1, 1)
        out_ref[s, kv, h] = rolled
        out_ref[s, kv, h, :, length - LANES:] = jnp.where(last, new, rolled[:, length - LANES:])

    heads = range(H_G)
    q = [pick(q_ref, h * HEAD_DIM) for h in heads]
    k_new = [pick(new_ref, h * HEAD_DIM) for h in heads]
    v_new = [pick(new_ref, GROUP_WIDTH + h * HEAD_DIM) for h in heads]
    shift(0, 0, k_new[0])
    shift(1, 0, v_new[0])
    sc = [jnp.where(used, jnp.sum(cache_ref[s, 0, h] * q[h], axis=0, keepdims=True), MASKED) for h in heads]
    s_new = [jnp.sum(k_new[h] * q[h], axis=0, keepdims=True) for h in heads]
    m = [jnp.maximum(jnp.max(sc[h], axis=1, keepdims=True), s_new[h]) for h in heads]
    shift(0, 1, k_new[1])
    shift(1, 1, v_new[1])
    p = [jnp.exp(sc[h] - m[h]) for h in heads]
    p_new = [jnp.exp(s_new[h] - m[h]) for h in heads]
    l = [jnp.sum(p[h], axis=1, keepdims=True) + p_new[h] for h in heads]
    shift(0, 2, k_new[2])
    shift(1, 2, v_new[2])
    o = [(jnp.sum(cache_ref[s, 1, h] * p[h], axis=1, keepdims=True) + v_new[h] * p_new[h]) / l[h]
         for h in heads]
    shift(0, 3, k_new[3])
    shift(1, 3, v_new[3])
    for h in heads:
        rows = slice(h * HEAD_DIM, (h + 1) * HEAD_DIM)
        o_ref[rows, :] = jnp.where(mine, o[h], o_ref[rows, :])
        lse_ref[rows, :] = jnp.where(mine, m[h] + jnp.log(l[h]), lse_ref[rows, :])


FFN_PARTS = 4
FUSED_VMEM_LIMIT = 57 * 1024 * 1024


def _ffn_cache_kernel(x_ref, p_ref, gf_ref, wfi_ref, wfo_ref, gp_ref, wpg_ref, wpp_ref, gfin_ref, *refs):
    ng = N_GROUPS
    q_refs, new_refs, cache_refs = refs[0:ng], refs[ng:2 * ng], refs[2 * ng:3 * ng]
    y_ref = refs[3 * ng]
    out_refs, o_refs, lse_refs = refs[3 * ng + 1:4 * ng + 1], refs[4 * ng + 1:5 * ng + 1], refs[5 * ng + 1:6 * ng + 1]
    (hb_ref,) = refs[6 * ng + 1:]
    acc_ref = y_ref
    tile = pl.program_id(0)
    part = pl.program_id(1)

    @pl.when((tile == 0) & (part == 0))
    def _():
        for r in o_refs + lse_refs:
            r[...] = jnp.zeros_like(r)

    n_chunks = D_FF // FF_CHUNK
    bounds = [(n_chunks + 2) * j // FFN_PARTS for j in range(FFN_PARTS)] + [n_chunks]

    for j in range(FFN_PARTS):
        @pl.when(part == j)
        def _(j=j):
            if j == 0:
                hb_ref[...] = _rms(x_ref[...], gf_ref[...]).astype(BF16)
                acc_ref[...] = jnp.zeros_like(acc_ref)
            _ffn_chunks(hb_ref[...], wfi_ref, wfo_ref, acc_ref, bounds[j], bounds[j + 1])
            if j == FFN_PARTS - 1:
                _ffn_finish(x_ref[...], acc_ref[...], p_ref, gp_ref, wpg_ref, wpp_ref, gfin_ref, y_ref)
            for g in reversed(range(ng)):
                _cache_sample(q_refs[g], new_refs[g], cache_refs[g], out_refs[g], o_refs[g], lse_refs[g],
                              0, tile * FFN_PARTS + part, GROUPS[g][1])


def _ffn_cache(x1, p, gf, wfi, wfo, gp, wpg, wpp, gfin, q_t, news, caches):
    tm = PROMPT_TM
    rows = x1.shape[0]
    n = caches[0].shape[0]
    assert (rows // tm) * FFN_PARTS == n
    row = lambda w: pl.BlockSpec((tm, w), lambda i, k: (i, 0))
    bigs = [pl.BlockSpec((1,) + c.shape[1:], lambda i, k: (i * FFN_PARTS + k, 0, 0, 0, 0)) for c in caches]
    res = pl.BlockSpec((GROUP_WIDTH, n), lambda i, k: (0, 0))
    res_shape = jax.ShapeDtypeStruct((GROUP_WIDTH, n), F32)
    ng = N_GROUPS
    return pl.pallas_call(
        _ffn_cache_kernel,
        grid=(rows // tm, FFN_PARTS),
        in_specs=[row(D_MODEL), row(PLE_DIM), _const_spec((1, D_MODEL)),
                  _const_spec((D_MODEL, 2 * D_FF)), _const_spec((D_FF, D_MODEL)),
                  _const_spec((1, D_MODEL)), _const_spec((D_MODEL, D_MODEL)),
                  _const_spec((PLE_DIM, D_MODEL)), _const_spec((1, D_MODEL))]
                 + [pl.BlockSpec((GROUP_WIDTH, n), lambda i, k, g=g: (g, 0), pipeline_mode=pl.Buffered(1))
                    for g in range(ng)]
                 + [_const_spec((2 * GROUP_WIDTH, n))] * ng + bigs,
        out_specs=[row(D_MODEL)] + bigs + [res] * (2 * ng),
        out_shape=[jax.ShapeDtypeStruct((rows, D_MODEL), F32)]
                  + [jax.ShapeDtypeStruct(c.shape, F32) for c in caches] + [res_shape] * (2 * ng),
        scratch_shapes=[pltpu.VMEM((tm, D_MODEL), BF16)],
        compiler_params=pltpu.CompilerParams(dimension_semantics=("arbitrary", "arbitrary"),
                                             vmem_limit_bytes=FUSED_VMEM_LIMIT),
        name="ffn_ple_caches",
    )(x1, p, gf, wfi, wfo, gp, wpg, wpp, gfin, *([q_t] * ng), *news, *caches)


def _rope_tables(pos):
    half = ROT_DIM // 2
    inv_freq = ROPE_THETA ** (-jnp.arange(half, dtype=F32) / half)
    ang = pos.astype(F32)[:, None] * inv_freq[None, :]
    cos, sin = jnp.cos(ang), jnp.sin(ang)
    n = pos.shape[0]
    zeros = lambda w: jnp.zeros((n, w), F32)
    cos_h = jnp.concatenate([cos, cos, jnp.ones((n, HEAD_DIM - ROT_DIM), F32)], axis=-1)
    sa_h = jnp.concatenate([-sin, zeros(HEAD_DIM - half)], axis=-1)
    sb_h = jnp.concatenate([zeros(half), sin, zeros(HEAD_DIM - ROT_DIM)], axis=-1)
    return tuple(jnp.tile(t, (1, H_G)) for t in (cos_h, sa_h, sb_h))


def _position_major(t):
    b, _, length = t.shape
    return jnp.transpose(t.reshape(b, 2, H_G, HEAD_DIM, length), (0, 4, 1, 2, 3))[None]


def kernel(x_prompt, x_sample, state_conv, cache_win_a, cache_win_b, cache_win_c, p_prompt, p_sample,
           w_in, g_mix, w_dw, b_dw, ln_g, ln_b, w_conv_out, w_attn_out, w_o, g_ffn, w_ffn_in, w_ffn_out,
           g_ple, w_ple_gate, w_ple_proj, g_final):
    assert w_in.shape[0] == 1, "single-layer step"
    w_in_b = w_in[0].astype(BF16)
    wco = w_conv_out[0].astype(BF16)
    wao = w_attn_out[0].astype(BF16)
    wo = w_o[0].astype(BF16)
    wfi = w_ffn_in[0].astype(BF16)
    wfo = w_ffn_out[0].astype(BF16)
    wpg = w_ple_gate[0].astype(BF16)
    wpp = w_ple_proj[0].astype(BF16)
    gfin = g_final.reshape(1, D_MODEL)
    n_prompt = BATCH * SEQ
    ctx = CONV_WIDTH - 1

    xp = x_prompt.reshape(n_prompt, D_MODEL)
    tabs_p = _rope_tables(jnp.arange(SEQ, dtype=jnp.int32))
    c_slabs, q, kva, kvb, kvc, gates, tail_a, tail_b, tail_c, u_tail = _in_proj_prompt(
        xp, g_mix, w_in_b, *tabs_p, w_dw[0], b_dw)
    per_batch = lambda t: t.reshape(BATCH, SEQ, t.shape[-1])
    o = _attention(per_batch(q), per_batch(kva), per_batch(kvb), per_batch(kvc))
    x1 = _prompt_mix(c_slabs, xp, o.reshape(n_prompt, GROUP_WIDTH), gates, ln_g, ln_b, wco, wao, wo)
    new_conv_prompt = u_tail[None, :, CONV_HALO - ctx:, :]

    n_s = DEC_BATCH
    xs = x_sample.reshape(n_s, D_MODEL)
    tabs_s = tuple(jnp.broadcast_to(t, (n_s, GROUP_WIDTH))
                   for t in _rope_tables(jnp.full((1,), PAST_LEN, jnp.int32)))
    u_s, q_t, new_a, new_b, new_c, gates_s = _in_proj_sample(xs, g_mix, w_in_b, *tabs_s)
    c_s, new_conv_t = _sample_conv(jnp.transpose(state_conv[0], (1, 0, 2)), u_s, w_dw[0], b_dw)
    caches_t = [jnp.transpose(c[0], (0, 2, 3, 4, 1)) for c in (cache_win_a, cache_win_b, cache_win_c)]
    fused = _ffn_cache(x1, p_prompt.reshape(n_prompt, PLE_DIM), g_ffn, wfi, wfo, g_ple, wpg, wpp, gfin,
                       q_t, [new_a, new_b, new_c], caches_t)
    y_prompt = fused[0]
    new_wins = [jnp.transpose(t, (0, 4, 1, 2, 3))[None] for t in fused[1:1 + N_GROUPS]]
    ogs = fused[1 + N_GROUPS:1 + 2 * N_GROUPS]
    lgs = fused[1 + 2 * N_GROUPS:]
    x1_s = _sample_mix(c_s, xs, ogs, lgs, gates_s, ln_g, ln_b, wco, wao, wo)
    y_sample = _ffn(x1_s, p_sample.reshape(n_s, PLE_DIM), g_ffn, wfi, wfo, g_ple, wpg, wpp, gfin,
                    n_s).reshape(n_s, 1, D_MODEL)
    new_conv_sample = jnp.transpose(new_conv_t, (1, 0, 2))[None]

    return (y_prompt.reshape(BATCH, SEQ, D_MODEL), y_sample, new_conv_prompt, _position_major(tail_a), _position_major(tail_b),
            _position_major(tail_c), new_conv_sample, new_wins[0], new_wins[1], new_wins[2])
```

```python
import jax
import jax.numpy as jnp
from jax import lax
from jax.experimental import pallas as pl
from jax.experimental.pallas import tpu as pltpu

D_MODEL = 1024
BATCH = 8
SEQ = 2048
DEC_BATCH = 128
PAST_LEN = 8192
HEAD_DIM = 64
GROUPS = ((128, 1), (512, 4), (2048, 16))
H_G = 4
N_GROUPS = len(GROUPS)
GROUP_WIDTH = H_G * HEAD_DIM
ATTN_WIDTH = N_GROUPS * GROUP_WIDTH
ROT_DIM = HEAD_DIM // 4
ROPE_THETA = 500000.0
C_CONV = D_MODEL
CONV_WIDTH = 31
D_FF = 2816
PLE_DIM = 256
NORM_EPS = 1e-6
NK = 128
Q_OFF = 2 * C_CONV
K_OFF = Q_OFF + ATTN_WIDTH
V_OFF = K_OFF + ATTN_WIDTH
GATE_OFF = V_OFF + ATTN_WIDTH
IN_COLS = GATE_OFF + 2 * D_MODEL

F32 = jnp.float32
BF16 = jnp.bfloat16
LANES = 128
SUBLANES = 8
N_SLABS = C_CONV // LANES
MASKED = -1e30
VMEM_LIMIT = 52 * 1024 * 1024


def _const_spec(shape):
    return pl.BlockSpec(shape, lambda *_: (0,) * len(shape), pipeline_mode=pl.Buffered(1))


def _params(n_axes):
    return pltpu.CompilerParams(dimension_semantics=("arbitrary",) * n_axes,
                                vmem_limit_bytes=VMEM_LIMIT)


def _rms(x, g):
    return x * lax.rsqrt(jnp.mean(x * x, axis=-1, keepdims=True) + NORM_EPS) * g


PROMPT_TM = 512
TILES_PER_SEQ = SEQ // PROMPT_TM
CONV_HALO = 32
CONV_PITCH = 4
CONV_GROUP = CONV_PITCH * SUBLANES
CONV_UNROLL = 1


def _projector(x_ref, g_ref, w_ref, cos_ref, sa_ref, sb_ref):
    hb = _rms(x_ref[...], g_ref[...]).astype(BF16)
    ch = GROUP_WIDTH

    def mm(c0):
        return jnp.dot(hb, w_ref[:, c0:c0 + ch], preferred_element_type=F32)

    def glu(c):
        return mm(c) * jax.nn.sigmoid(mm(C_CONV + c))

    def gates(gate_ref, lo=0, hi=2 * D_MODEL // ch):
        for c in range(lo * ch, hi * ch, ch):
            gate_ref[:, c:c + ch] = jax.nn.sigmoid(mm(GATE_OFF + c)).astype(BF16)

    def rope(z):
        return (z * cos_ref[...] + pltpu.roll(z, ch - ROT_DIM // 2, 1) * sa_ref[...]
                + pltpu.roll(z, ROT_DIM // 2, 1) * sb_ref[...])

    def qkv(g):
        c = g * ch
        return rope(mm(Q_OFF + c)) * (HEAD_DIM ** -0.5), rope(mm(K_OFF + c)), mm(V_OFF + c)

    return qkv, glu, gates


def _conv_slab(win_ref, lc, wdw_ref, bdw_ref, c_ref, n_rows):
    first = CONV_HALO - (CONV_WIDTH - 1)
    cols = slice(lc * LANES, (lc + 1) * LANES)
    n_acc = CONV_UNROLL * CONV_PITCH
    offs = [(i // CONV_PITCH) * CONV_GROUP + i % CONV_PITCH for i in range(n_acc)]
    bias = jnp.broadcast_to(bdw_ref[:, cols], (SUBLANES, LANES))
    for base in range(0, n_rows, CONV_UNROLL * CONV_GROUP):
        accs = [bias] * n_acc
        for k in range(CONV_WIDTH):
            wk = wdw_ref[k:k + 1, cols]
            for i in range(n_acc):
                tap = win_ref[lc, pl.ds(base + first + k + offs[i], SUBLANES, stride=CONV_PITCH), :]
                accs[i] = accs[i] + tap * wk
        for i in range(n_acc):
            c_ref[lc, pl.ds(base + offs[i], SUBLANES, stride=CONV_PITCH), :] = accs[i]


def _in_proj_prompt_kernel(x_ref, g_ref, w_ref, cos_ref, sa_ref, sb_ref, wdw_ref, bdw_ref,
                           c_ref, q_ref, kva_ref, kvb_ref, kvc_ref, gate_ref, ta_ref, tb_ref, tc_ref, ut_ref,
                           win_ref):
    tm = PROMPT_TM
    ch = GROUP_WIDTH

    @pl.when(pl.program_id(0) == 0)
    def _():
        win_ref[:, tm:tm + CONV_HALO, :] = jnp.zeros((N_SLABS, CONV_HALO, LANES), F32)

    qkv, glu, gates = _projector(x_ref, g_ref, w_ref, cos_ref, sa_ref, sb_ref)
    first_tile = pl.program_id(0) % TILES_PER_SEQ == 0
    def head_group(g):
        kv_ref, t_ref = ((kva_ref, ta_ref), (kvb_ref, tb_ref), (kvc_ref, tc_ref))[g]
        q, k, v = qkv(g)
        q_ref[:, g * ch:(g + 1) * ch] = q
        kv_ref[:, 0:ch] = k
        kv_ref[:, ch:2 * ch] = v
        n = t_ref.shape[1]
        t_ref[0:ch, :] = k[tm - n:, :].T
        t_ref[ch:2 * ch, :] = v[tm - n:, :].T

    def glu_chunk(i):
        c = i * ch
        u = glu(c)
        ut_ref[:, c:c + ch] = u[tm - CONV_HALO:, :]
        for j in range(ch // LANES):
            lc = c // LANES + j
            win_ref[lc, 0:CONV_HALO, :] = jnp.where(first_tile, 0.0, win_ref[lc, tm:tm + CONV_HALO, :])
            win_ref[lc, CONV_HALO:CONV_HALO + tm, :] = u[:, j * LANES:(j + 1) * LANES]

    n_gate = 2 * D_MODEL // ch
    after_slab = [
        lambda: glu_chunk(1), lambda: head_group(0), lambda: glu_chunk(2), lambda: head_group(1),
        lambda: glu_chunk(3), lambda: head_group(2),
        lambda: gates(gate_ref, 0, n_gate // 2), lambda: gates(gate_ref, n_gate // 2, n_gate)]
    glu_chunk(0)
    for lc in range(N_SLABS):
        _conv_slab(win_ref, lc, wdw_ref, bdw_ref, c_ref, tm)
        after_slab[lc]()


def _in_proj_prompt(x2d, g, w, cos, sa, sb, wdw, bdw):
    tm = PROMPT_TM
    rows = x2d.shape[0]
    assert GROUPS[1][0] == tm and GROUPS[2][0] == SEQ and GROUPS[0][0] <= tm
    row = lambda n: pl.BlockSpec((tm, n), lambda i: (i, 0))
    tab = pl.BlockSpec((tm, GROUP_WIDTH), lambda i: (i % TILES_PER_SEQ, 0))
    kv_w = 2 * GROUP_WIDTH
    tail = lambda n: pl.BlockSpec((None, kv_w, n), lambda i: (i // TILES_PER_SEQ, 0, 0))
    return pl.pallas_call(
        _in_proj_prompt_kernel,
        grid=(rows // tm,),
        in_specs=[row(D_MODEL), _const_spec((1, D_MODEL)), _const_spec((D_MODEL, IN_COLS)), tab, tab, tab,
                  _const_spec((CONV_WIDTH, C_CONV)), _const_spec((1, C_CONV))],
        out_specs=[pl.BlockSpec((N_SLABS, tm, LANES), lambda i: (0, i, 0)),
                   row(ATTN_WIDTH), row(kv_w), row(kv_w), row(kv_w), row(2 * D_MODEL),
                   tail(GROUPS[0][0]), tail(GROUPS[1][0]),
                   pl.BlockSpec((None, kv_w, tm), lambda i: (i // TILES_PER_SEQ, 0, i % TILES_PER_SEQ)),
                   pl.BlockSpec((None, CONV_HALO, C_CONV), lambda i: (i // TILES_PER_SEQ, 0, 0))],
        out_shape=[jax.ShapeDtypeStruct((N_SLABS, rows, LANES), F32),
                   jax.ShapeDtypeStruct((rows, ATTN_WIDTH), F32),
                   jax.ShapeDtypeStruct((rows, kv_w), F32),
                   jax.ShapeDtypeStruct((rows, kv_w), F32),
                   jax.ShapeDtypeStruct((rows, kv_w), F32),
                   jax.ShapeDtypeStruct((rows, 2 * D_MODEL), BF16),
                   jax.ShapeDtypeStruct((BATCH, kv_w, GROUPS[0][0]), F32),
                   jax.ShapeDtypeStruct((BATCH, kv_w, GROUPS[1][0]), F32),
                   jax.ShapeDtypeStruct((BATCH, kv_w, SEQ), F32),
                   jax.ShapeDtypeStruct((BATCH, CONV_HALO, C_CONV), F32)],
        scratch_shapes=[pltpu.VMEM((N_SLABS, CONV_HALO + tm, LANES), F32)],
        compiler_params=_params(1),
        name="in_proj_prompt",
    )(x2d, g, w, cos, sa, sb, wdw, bdw)


def _in_proj_sample_kernel(x_ref, g_ref, w_ref, cos_ref, sa_ref, sb_ref,
                           u_ref, qt_ref, ta_ref, tb_ref, tc_ref, gate_ref):
    qkv, glu, gates = _projector(x_ref, g_ref, w_ref, cos_ref, sa_ref, sb_ref)
    ch = GROUP_WIDTH
    for g, t_ref in enumerate((ta_ref, tb_ref, tc_ref)):
        q, k, v = qkv(g)
        qt_ref[g * ch:(g + 1) * ch, :] = q.T
        t_ref[0:ch, :] = k.T
        t_ref[ch:2 * ch, :] = v.T
    for c in range(0, C_CONV, ch):
        u = glu(c)
        for j in range(ch // LANES):
            u_ref[c // LANES + j] = u[:, j * LANES:(j + 1) * LANES]
    gates(gate_ref)


def _in_proj_sample(x2d, g, w, cos, sa, sb):
    n = x2d.shape[0]
    full = lambda r, c: pl.BlockSpec((r, c), lambda i: (0, 0))
    kv_w = 2 * GROUP_WIDTH
    return pl.pallas_call(
        _in_proj_sample_kernel,
        grid=(1,),
        in_specs=[full(n, D_MODEL), _const_spec((1, D_MODEL)), _const_spec((D_MODEL, IN_COLS)),
                  full(n, GROUP_WIDTH), full(n, GROUP_WIDTH), full(n, GROUP_WIDTH)],
        out_specs=[pl.BlockSpec((N_SLABS, n, LANES), lambda i: (0, 0, 0)),
                   full(ATTN_WIDTH, n), full(kv_w, n), full(kv_w, n), full(kv_w, n), full(n, 2 * D_MODEL)],
        out_shape=[jax.ShapeDtypeStruct((N_SLABS, n, LANES), F32),
                   jax.ShapeDtypeStruct((ATTN_WIDTH, n), F32),
                   jax.ShapeDtypeStruct((kv_w, n), F32),
                   jax.ShapeDtypeStruct((kv_w, n), F32),
                   jax.ShapeDtypeStruct((kv_w, n), F32),
                   jax.ShapeDtypeStruct((n, 2 * D_MODEL), BF16)],
        compiler_params=_params(1),
        name="in_proj_sample",
    )(x2d, g, w, cos, sa, sb)


ATTN_UNROLL = 8


def _combine_groups(lses, outs):
    mx = jnp.maximum(jnp.maximum(lses[0], lses[1]), lses[2])
    es = [jnp.exp(l - mx) for l in lses]
    num = es[0] * outs[0] + es[1] * outs[1] + es[2] * outs[2]
    return num / (es[0] + es[1] + es[2])


def _largest_divisor(n, cap):
    return max(d for d in range(1, cap + 1) if n % d == 0)


def _attn_kernel(qa_ref, qb_ref, qc_ref, ka_ref, va_ref, kb_ref, vb_ref, kc_ref, vc_ref,
                 o_ref, og_ref, lg_ref):
    lane = lax.broadcasted_iota(jnp.int32, (NK, LANES), 1)
    qi = lax.broadcasted_iota(jnp.int32, (NK, LANES), 0)
    first_head = lane < HEAD_DIM
    own_mask = lane <= qi
    prev_mask = lane >= qi
    nt = (((1,), (1,)), ((), ()))

    def blocks(g, q_ref, k_ref, v_ref, row_list):
        scores, values = [], []
        for rows, prev_rows in row_list:
            qb = q_ref[rows, :]
            q2 = jnp.concatenate([jnp.where(first_head, qb, 0.0), jnp.where(first_head, 0.0, qb)],
                                 axis=0).astype(BF16)
            keys = k_ref[rows, :].astype(BF16)
            vals = v_ref[rows, :].astype(BF16)
            mask = own_mask
            if prev_rows is not None:
                keys = jnp.concatenate([keys, k_ref[prev_rows, :].astype(BF16)], axis=0)
                vals = jnp.concatenate([vals, v_ref[prev_rows, :].astype(BF16)], axis=0)
                mask = jnp.concatenate([own_mask, prev_mask], axis=1)
            values.append(jnp.concatenate([vals, jnp.ones_like(vals)], axis=1))
            s = lax.dot_general(q2, keys, nt, preferred_element_type=F32)
            scores.append(jnp.where(jnp.concatenate([mask, mask], axis=0), s, MASKED))
        probs, maxes = [], []
        for s in scores:
            m = jnp.max(s, axis=-1, keepdims=True)
            probs.append(jnp.exp(s - m).astype(BF16))
            maxes.append(m)
        for b, (rows, _) in enumerate(row_list):
            r = jnp.dot(probs[b], values[b], preferred_element_type=F32)
            l = r[:, LANES:]
            out = r[:, :LANES] / l
            lse = maxes[b] + jnp.log(l)
            og_ref[g, rows, :] = jnp.where(first_head, out[:NK], out[NK:])
            lg_ref[g, rows, :] = jnp.where(first_head, lse[:NK], lse[NK:])

    for g, (q_ref, k_ref, v_ref) in enumerate(((qa_ref, ka_ref, va_ref), (qb_ref, kb_ref, vb_ref),
                                               (qc_ref, kc_ref, vc_ref))):
        dil = GROUPS[g][1]
        span = NK * dil
        n_later = dil * (SEQ // span - 1)

        def rows_at(start, dil=dil):
            return pl.ds(start, NK) if dil == 1 else pl.ds(start, NK, stride=dil)

        n_first = _largest_divisor(dil, ATTN_UNROLL)

        def first(t, carry, g=g, q_ref=q_ref, k_ref=k_ref, v_ref=v_ref, rows_at=rows_at, n=n_first):
            blocks(g, q_ref, k_ref, v_ref, [(rows_at(t * n + j), None) for j in range(n)])
            return carry

        lax.fori_loop(0, dil // n_first, first, 0)
        if n_later:
            n_per = _largest_divisor(n_later, ATTN_UNROLL)

            def later(t, carry, g=g, q_ref=q_ref, k_ref=k_ref, v_ref=v_ref, rows_at=rows_at, dil=dil,
                      span=span, n=n_per):
                row_list = []
                for j in range(n):
                    i = t * n + j
                    start = i % dil + (1 + i // dil) * span
                    row_list.append((rows_at(start), rows_at(start - span)))
                blocks(g, q_ref, k_ref, v_ref, row_list)
                return carry

            lax.fori_loop(0, n_later // n_per, later, 0)

    tc = 256
    for t in range(0, SEQ, tc):
        rows = slice(t, t + tc)
        o_ref[rows, :] = _combine_groups([lg_ref[g, rows, :] for g in range(N_GROUPS)],
                                         [og_ref[g, rows, :] for g in range(N_GROUPS)]).astype(o_ref.dtype)


def _attention(q3, kva3, kvb3, kvc3):
    def col(c):
        return pl.BlockSpec((None, SEQ, LANES), lambda b, hp, c=c: (b, 0, c + hp))
    n_pairs = GROUP_WIDTH // LANES
    return pl.pallas_call(
        _attn_kernel,
        grid=(BATCH, n_pairs),
        in_specs=[col(0), col(n_pairs), col(2 * n_pairs),
                  col(0), col(n_pairs), col(0), col(n_pairs), col(0), col(n_pairs)],
        out_specs=pl.BlockSpec((None, SEQ, LANES), lambda b, hp: (b, 0, hp)),
        out_shape=jax.ShapeDtypeStruct((BATCH, SEQ, GROUP_WIDTH), BF16),
        scratch_shapes=[pltpu.VMEM((N_GROUPS, SEQ, LANES), F32), pltpu.VMEM((N_GROUPS, SEQ, LANES), F32)],
        compiler_params=_params(2),
        name="prompt_attention",
    )(q3, q3, q3, kva3, kva3, kvb3, kvb3, kvc3, kvc3)


def _merge(c, x, ob, gate_ref, lng_ref, lnb_ref, wco_ref, wao_ref, wo_ref):
    mu = jnp.mean(c, axis=-1, keepdims=True)
    d = c - mu
    var = jnp.mean(d * d, axis=-1, keepdims=True)
    y = d * lax.rsqrt(var + NORM_EPS) * lng_ref[...] + lnb_ref[...]
    a_out = jnp.dot(jax.nn.silu(y).astype(BF16), wco_ref[...], preferred_element_type=F32)
    b_out = jnp.dot(ob, wao_ref[...], preferred_element_type=F32)
    merged = (gate_ref[:, 0:D_MODEL].astype(F32) * a_out
              + gate_ref[:, D_MODEL:2 * D_MODEL].astype(F32) * b_out)
    return x + jnp.dot(merged.astype(BF16), wo_ref[...], preferred_element_type=F32)


MIX_TM = 512


def _prompt_mix_kernel(c_ref, x_ref, o_ref, gate_ref, lng_ref, lnb_ref, wco_ref, wao_ref, wo_ref, x1_ref):
    c = jnp.concatenate([c_ref[lc] for lc in range(N_SLABS)], axis=-1)
    x1_ref[...] = _merge(c, x_ref[...], o_ref[...], gate_ref, lng_ref, lnb_ref, wco_ref, wao_ref, wo_ref)


def _prompt_mix(c_slabs, x2d, o2d, gate2d, lng, lnb, wco, wao, wo):
    tm = MIX_TM
    rows = x2d.shape[0]
    row = lambda n: pl.BlockSpec((tm, n), lambda i: (i, 0))
    return pl.pallas_call(
        _prompt_mix_kernel,
        grid=(rows // tm,),
        in_specs=[pl.BlockSpec((N_SLABS, tm, LANES), lambda i: (0, i, 0)),
                  row(D_MODEL), row(GROUP_WIDTH), row(2 * D_MODEL),
                  _const_spec((1, C_CONV)), _const_spec((1, C_CONV)),
                  _const_spec((C_CONV, D_MODEL)), _const_spec((GROUP_WIDTH, D_MODEL)),
                  _const_spec((D_MODEL, D_MODEL))],
        out_specs=row(D_MODEL),
        out_shape=jax.ShapeDtypeStruct((rows, D_MODEL), F32),
        compiler_params=_params(1),
        name="prompt_mix",
    )(c_slabs, x2d, o2d, gate2d, lng, lnb, wco, wao, wo)


def _sample_mix_kernel(c_ref, x_ref, oa_ref, ob_ref, oc_ref, la_ref, lb_ref, lc_ref, gate_ref,
                       lng_ref, lnb_ref, wco_ref, wao_ref, wo_ref, x1_ref):
    ob = _combine_groups([r[...].T for r in (la_ref, lb_ref, lc_ref)],
                         [r[...].T for r in (oa_ref, ob_ref, oc_ref)]).astype(BF16)
    x1_ref[...] = _merge(c_ref[...], x_ref[...], ob, gate_ref, lng_ref, lnb_ref, wco_ref, wao_ref, wo_ref)


def _sample_mix(c, x, ogs, lgs, gate, lng, lnb, wco, wao, wo):
    rows = x.shape[0]
    full = lambda n: _const_spec((rows, n))
    per_group = _const_spec((GROUP_WIDTH, rows))
    return pl.pallas_call(
        _sample_mix_kernel,
        grid=(1,),
        in_specs=[full(C_CONV), full(D_MODEL)] + [per_group] * (2 * N_GROUPS) + [full(2 * D_MODEL),
                  _const_spec((1, C_CONV)), _const_spec((1, C_CONV)),
                  _const_spec((C_CONV, D_MODEL)), _const_spec((GROUP_WIDTH, D_MODEL)),
                  _const_spec((D_MODEL, D_MODEL))],
        out_specs=pl.BlockSpec((rows, D_MODEL), lambda i: (0, 0)),
        out_shape=jax.ShapeDtypeStruct((rows, D_MODEL), F32),
        compiler_params=_params(1),
        name="sample_mix",
    )(c, x, *ogs, *lgs, gate, lng, lnb, wco, wao, wo)


FF_CHUNK = 256


def _ffn_chunks(hb, wfi_ref, wfo_ref, acc_ref, lo, hi):
    acts = []
    for c in range(lo * FF_CHUNK, hi * FF_CHUNK, FF_CHUNK):
        gch = jnp.dot(hb, wfi_ref[:, c:c + FF_CHUNK], preferred_element_type=F32)
        uch = jnp.dot(hb, wfi_ref[:, D_FF + c:D_FF + c + FF_CHUNK], preferred_element_type=F32)
        acts.append((jax.nn.silu(gch) * uch).astype(BF16))
    act = jnp.concatenate(acts, axis=1) if len(acts) > 1 else acts[0]
    acc_ref[...] += jnp.dot(act, wfo_ref[lo * FF_CHUNK:hi * FF_CHUNK, :], preferred_element_type=F32)


def _ffn_finish(x1, ffn_out, p_ref, gp_ref, wpg_ref, wpp_ref, gfin_ref, y_ref):
    x2 = x1 + ffn_out
    hp = _rms(x2, gp_ref[...]).astype(BF16)
    gate = jax.nn.sigmoid(jnp.dot(hp, wpg_ref[...], preferred_element_type=F32))
    pe = jnp.dot(p_ref[...].astype(BF16), wpp_ref[...], preferred_element_type=F32)
    x3 = x2 + gate * pe
    y_ref[...] = _rms(x3, gfin_ref[...])


def _ffn_kernel(x_ref, p_ref, gf_ref, wfi_ref, wfo_ref, gp_ref, wpg_ref, wpp_ref, gfin_ref, y_ref, acc_ref):
    x1 = x_ref[...]
    hb = _rms(x1, gf_ref[...]).astype(BF16)
    acc_ref[...] = jnp.zeros_like(acc_ref)
    _ffn_chunks(hb, wfi_ref, wfo_ref, acc_ref, 0, D_FF // FF_CHUNK)
    _ffn_finish(x1, acc_ref[...], p_ref, gp_ref, wpg_ref, wpp_ref, gfin_ref, y_ref)


def _ffn(x1, p, gf, wfi, wfo, gp, wpg, wpp, gfin, tm):
    rows = x1.shape[0]
    row = lambda n: pl.BlockSpec((tm, n), lambda i: (i, 0))
    return pl.pallas_call(
        _ffn_kernel,
        grid=(rows // tm,),
        in_specs=[row(D_MODEL), row(PLE_DIM), _const_spec((1, D_MODEL)),
                  _const_spec((D_MODEL, 2 * D_FF)), _const_spec((D_FF, D_MODEL)),
                  _const_spec((1, D_MODEL)), _const_spec((D_MODEL, D_MODEL)),
                  _const_spec((PLE_DIM, D_MODEL)), _const_spec((1, D_MODEL))],
        out_specs=row(D_MODEL),
        out_shape=jax.ShapeDtypeStruct((rows, D_MODEL), F32),
        scratch_shapes=[pltpu.VMEM((tm, D_MODEL), F32)],
        compiler_params=_params(1),
        name="ffn_ple",
    )(x1, p, gf, wfi, wfo, gp, wpg, wpp, gfin)


SAMPLE_CONV_BLOCK = 32


def _sample_conv_kernel(state_ref, u_ref, wdw_ref, bdw_ref, c_ref, new_ref):
    ctx = CONV_WIDTH - 1
    for lc in range(N_SLABS):
        cols = slice(lc * LANES, (lc + 1) * LANES)
        u = u_ref[lc]
        acc = u * wdw_ref[ctx:ctx + 1, cols] + bdw_ref[:, cols]
        for k in range(ctx):
            acc = acc + state_ref[k, :, cols] * wdw_ref[k:k + 1, cols]
        c_ref[:, cols] = acc
        new_ref[ctx - 1, :, cols] = u
    for k in range(ctx - 1):
        new_ref[k] = state_ref[k + 1]


def _sample_conv(state_t, u_slabs, wdw, bdw):
    ctx, n, _ = state_t.shape
    sb = SAMPLE_CONV_BLOCK
    return pl.pallas_call(
        _sample_conv_kernel,
        grid=(n // sb,),
        in_specs=[pl.BlockSpec((ctx, sb, C_CONV), lambda i: (0, i, 0)),
                  pl.BlockSpec((N_SLABS, sb, LANES), lambda i: (0, i, 0)),
                  _const_spec((CONV_WIDTH, C_CONV)), _const_spec((1, C_CONV))],
        out_specs=[pl.BlockSpec((sb, C_CONV), lambda i: (i, 0)),
                   pl.BlockSpec((ctx, sb, C_CONV), lambda i: (0, i, 0))],
        out_shape=[jax.ShapeDtypeStruct((n, C_CONV), F32),
                   jax.ShapeDtypeStruct((ctx, n, C_CONV), F32)],
        compiler_params=_params(1),
        name="sample_conv",
    )(state_t, u_slabs, wdw, bdw)


def _cache_sample(q_ref, new_ref, cache_ref, out_ref, o_ref, lse_ref, s, sample, dil):
    length = cache_ref.shape[-1]
    pos = lax.broadcasted_iota(jnp.int32, (1, length), 1)
    used = (pos & (dil - 1)) == 0
    lane = lax.broadcasted_iota(jnp.int32, (HEAD_DIM, LANES), 1)
    last = lane == LANES - 1
    mine = lane == sample

    def pick(ref, r0):
        return jnp.sum(jnp.where(mine, ref[r0:r0 + HEAD_DIM, :], 0.0), axis=1, keepdims=True)

    def shift(kv, h, new):
        rolled = pltpu.roll(cache_ref[s, kv, h], length - 1, 1)
        out_ref[s, kv, h] = rolled
        out_ref[s, kv, h, :, length - LANES:] = jnp.where(last, new, rolled[:, length - LANES:])

    heads = range(H_G)
    q = [pick(q_ref, h * HEAD_DIM) for h in heads]
    k_new = [pick(new_ref, h * HEAD_DIM) for h in heads]
    v_new = [pick(new_ref, GROUP_WIDTH + h * HEAD_DIM) for h in heads]
    shift(0, 0, k_new[0])
    shift(1, 0, v_new[0])
    sc = [jnp.where(used, jnp.sum(cache_ref[s, 0, h] * q[h], axis=0, keepdims=True), MASKED) for h in heads]
    s_new = [jnp.sum(k_new[h] * q[h], axis=0, keepdims=True) for h in heads]
    m = [jnp.maximum(jnp.max(sc[h], axis=1, keepdims=True), s_new[h]) for h in heads]
    shift(0, 1, k_new[1])
    shift(1, 1, v_new[1])
    p = [jnp.exp(sc[h] - m[h]) for h in heads]
    p_new = [jnp.exp(s_new[h] - m[h]) for h in heads]
    l = [jnp.sum(p[h], axis=1, keepdims=True) + p_new[h] for h in heads]
    shift(0, 2, k_new[2])
    shift(1, 2, v_new[2])
    o = [(jnp.sum(cache_ref[s, 1, h] * p[h], axis=1, keepdims=True) + v_new[h] * p_new[h]) / l[h]
         for h in heads]
    shift(0, 3, k_new[3])
    shift(1, 3, v_new[3])
    for h in heads:
        rows = slice(h * HEAD_DIM, (h + 1) * HEAD_DIM)
        o_ref[rows, :] = jnp.where(mine, o[h], o_ref[rows, :])
        lse_ref[rows, :] = jnp.where(mine, m[h] + jnp.log(l[h]), lse_ref[rows, :])


FFN_PARTS = 4
FUSED_VMEM_LIMIT = 57 * 1024 * 1024


def _ffn_cache_kernel(x_ref, p_ref, gf_ref, wfi_ref, wfo_ref, gp_ref, wpg_ref, wpp_ref, gfin_ref, *refs):
    ng = N_GROUPS
    q_refs, new_refs, cache_refs = refs[0:ng], refs[ng:2 * ng], refs[2 * ng:3 * ng]
    y_ref = refs[3 * ng]
    out_refs, o_refs, lse_refs = refs[3 * ng + 1:4 * ng + 1], refs[4 * ng + 1:5 * ng + 1], refs[5 * ng + 1:6 * ng + 1]
    (hb_ref,) = refs[6 * ng + 1:]
    acc_ref = y_ref
    tile = pl.program_id(0)
    part = pl.program_id(1)

    @pl.when((tile == 0) & (part == 0))
    def _():
        for r in o_refs + lse_refs:
            r[...] = jnp.zeros_like(r)

    n_chunks = D_FF // FF_CHUNK
    bounds = [(n_chunks + 2) * j // FFN_PARTS for j in range(FFN_PARTS)] + [n_chunks]

    for j in range(FFN_PARTS):
        @pl.when(part == j)
        def _(j=j):
            if j == 0:
                hb_ref[...] = _rms(x_ref[...], gf_ref[...]).astype(BF16)
                acc_ref[...] = jnp.zeros_like(acc_ref)
            _ffn_chunks(hb_ref[...], wfi_ref, wfo_ref, acc_ref, bounds[j], bounds[j + 1])
            if j == FFN_PARTS - 1:
                _ffn_finish(x_ref[...], acc_ref[...], p_ref, gp_ref, wpg_ref, wpp_ref, gfin_ref, y_ref)
            for g in reversed(range(ng)):
                _cache_sample(q_refs[g], new_refs[g], cache_refs[g], out_refs[g], o_refs[g], lse_refs[g],
                              0, tile * FFN_PARTS + part, GROUPS[g][1])


def _ffn_cache(x1, p, gf, wfi, wfo, gp, wpg, wpp, gfin, q_t, news, caches):
    tm = PROMPT_TM
    rows = x1.shape[0]
    n = caches[0].shape[0]
    assert (rows // tm) * FFN_PARTS == n
    row = lambda w: pl.BlockSpec((tm, w), lambda i, k: (i, 0))
    bigs = [pl.BlockSpec((1,) + c.shape[1:], lambda i, k: (i * FFN_PARTS + k, 0, 0, 0, 0)) for c in caches]
    res = pl.BlockSpec((GROUP_WIDTH, n), lambda i, k: (0, 0))
    res_shape = jax.ShapeDtypeStruct((GROUP_WIDTH, n), F32)
    ng = N_GROUPS
    return pl.pallas_call(
        _ffn_cache_kernel,
        grid=(rows // tm, FFN_PARTS),
        in_specs=[row(D_MODEL), row(PLE_DIM), _const_spec((1, D_MODEL)),
                  _const_spec((D_MODEL, 2 * D_FF)), _const_spec((D_FF, D_MODEL)),
                  _const_spec((1, D_MODEL)), _const_spec((D_MODEL, D_MODEL)),
                  _const_spec((PLE_DIM, D_MODEL)), _const_spec((1, D_MODEL))]
                 + [pl.BlockSpec((GROUP_WIDTH, n), lambda i, k, g=g: (g, 0), pipeline_mode=pl.Buffered(1))
                    for g in range(ng)]
                 + [_const_spec((2 * GROUP_WIDTH, n))] * ng + bigs,
        out_specs=[row(D_MODEL)] + bigs + [res] * (2 * ng),
        out_shape=[jax.ShapeDtypeStruct((rows, D_MODEL), F32)]
                  + [jax.ShapeDtypeStruct(c.shape, F32) for c in caches] + [res_shape] * (2 * ng),
        scratch_shapes=[pltpu.VMEM((tm, D_MODEL), BF16)],
        compiler_params=pltpu.CompilerParams(dimension_semantics=("arbitrary", "arbitrary"),
                                             vmem_limit_bytes=FUSED_VMEM_LIMIT),
        name="ffn_ple_caches",
    )(x1, p, gf, wfi, wfo, gp, wpg, wpp, gfin, *([q_t] * ng), *news, *caches)


def _rope_tables(pos):
    half = ROT_DIM // 2
    inv_freq = ROPE_THETA ** (-jnp.arange(half, dtype=F32) / half)
    ang = pos.astype(F32)[:, None] * inv_freq[None, :]
    cos, sin = jnp.cos(ang), jnp.sin(ang)
    n = pos.shape[0]
    zeros = lambda w: jnp.zeros((n, w), F32)
    cos_h = jnp.concatenate([cos, cos, jnp.ones((n, HEAD_DIM - ROT_DIM), F32)], axis=-1)
    sa_h = jnp.concatenate([-sin, zeros(HEAD_DIM - half)], axis=-1)
    sb_h = jnp.concatenate([zeros(half), sin, zeros(HEAD_DIM - ROT_DIM)], axis=-1)
    return tuple(jnp.tile(t, (1, H_G)) for t in (cos_h, sa_h, sb_h))


def _position_major(t):
    b, _, length = t.shape
    return jnp.transpose(t.reshape(b, 2, H_G, HEAD_DIM, length), (0, 4, 1, 2, 3))[None]


def kernel(x_prompt, x_sample, state_conv, cache_win_a, cache_win_b, cache_win_c, p_prompt, p_sample,
           w_in, g_mix, w_dw, b_dw, ln_g, ln_b, w_conv_out, w_attn_out, w_o, g_ffn, w_ffn_in, w_ffn_out,
           g_ple, w_ple_gate, w_ple_proj, g_final):
    assert w_in.shape[0] == 1, "single-layer step"
    w_in_b = w_in[0].astype(BF16)
    wco = w_conv_out[0].astype(BF16)
    wao = w_attn_out[0].astype(BF16)
    wo = w_o[0].astype(BF16)
    wfi = w_ffn_in[0].astype(BF16)
    wfo = w_ffn_out[0].astype(BF16)
    wpg = w_ple_gate[0].astype(BF16)
    wpp = w_ple_proj[0].astype(BF16)
    gfin = g_final.reshape(1, D_MODEL)
    n_prompt = BATCH * SEQ
    ctx = CONV_WIDTH - 1

    xp = x_prompt.reshape(n_prompt, D_MODEL)
    tabs_p = _rope_tables(jnp.arange(SEQ, dtype=jnp.int32))
    c_slabs, q, kva, kvb, kvc, gates, tail_a, tail_b, tail_c, u_tail = _in_proj_prompt(
        xp, g_mix, w_in_b, *tabs_p, w_dw[0], b_dw)
    per_batch = lambda t: t.reshape(BATCH, SEQ, t.shape[-1])
    o = _attention(per_batch(q), per_batch(kva), per_batch(kvb), per_batch(kvc))
    x1 = _prompt_mix(c_slabs, xp, o.reshape(n_prompt, GROUP_WIDTH), gates, ln_g, ln_b, wco, wao, wo)
    new_conv_prompt = u_tail[None, :, CONV_HALO - ctx:, :]

    n_s = DEC_BATCH
    xs = x_sample.reshape(n_s, D_MODEL)
    tabs_s = tuple(jnp.broadcast_to(t, (n_s, GROUP_WIDTH))
                   for t in _rope_tables(jnp.full((1,), PAST_LEN, jnp.int32)))
    u_s, q_t, new_a, new_b, new_c, gates_s = _in_proj_sample(xs, g_mix, w_in_b, *tabs_s)
    c_s, new_conv_t = _sample_conv(jnp.transpose(state_conv[0], (1, 0, 2)), u_s, w_dw[0], b_dw)
    caches_t = [jnp.transpose(c[0], (0, 2, 3, 4, 1)) for c in (cache_win_a, cache_win_b, cache_win_c)]
    fused = _ffn_cache(x1, p_prompt.reshape(n_prompt, PLE_DIM), g_ffn, wfi, wfo, g_ple, wpg, wpp, gfin,
                       q_t, [new_a, new_b, new_c], caches_t)
    y_prompt = fused[0]
    new_wins = [jnp.transpose(t, (0, 4, 1, 2, 3))[None] for t in fused[1:1 + N_GROUPS]]
    ogs = fused[1 + N_GROUPS:1 + 2 * N_GROUPS]
    lgs = fused[1 + 2 * N_GROUPS:]
    x1_s = _sample_mix(c_s, xs, ogs, lgs, gates_s, ln_g, ln_b, wco, wao, wo)
    y_sample = _ffn(x1_s, p_sample.reshape(n_s, PLE_DIM), g_ffn, wfi, wfo, g_ple, wpg, wpp, gfin,
                    n_s).reshape(n_s, 1, D_MODEL)
    new_conv_sample = jnp.transpose(new_conv_t, (1, 0, 2))[None]

    return (y_prompt.reshape(BATCH, SEQ, D_MODEL), y_sample, new_conv_prompt, _position_major(tail_a), _position_major(tail_b),
            _position_major(tail_c), new_conv_sample, new_wins[0], new_wins[1], new_wins[2])
```

```python
import jax
import jax.numpy as jnp
from jax import lax
from jax.experimental import pallas as pl
from jax.experimental.pallas import tpu as pltpu

D_MODEL = 1024
BATCH = 8
SEQ = 2048
DEC_BATCH = 128
PAST_LEN = 8192
HEAD_DIM = 64
GROUPS = ((128, 1), (512, 4), (2048, 16))
H_G = 4
N_GROUPS = len(GROUPS)
GROUP_WIDTH = H_G * HEAD_DIM
ATTN_WIDTH = N_GROUPS * GROUP_WIDTH
ROT_DIM = HEAD_DIM // 4
ROPE_THETA = 500000.0
C_CONV = D_MODEL
CONV_WIDTH = 31
D_FF = 2816
PLE_DIM = 256
NORM_EPS = 1e-6
NK = 128
Q_OFF = 2 * C_CONV
K_OFF = Q_OFF + ATTN_WIDTH
V_OFF = K_OFF + ATTN_WIDTH
GATE_OFF = V_OFF + ATTN_WIDTH
IN_COLS = GATE_OFF + 2 * D_MODEL

F32 = jnp.float32
BF16 = jnp.bfloat16
LANES = 128
SUBLANES = 8
N_SLABS = C_CONV // LANES
MASKED = -jnp.inf
VMEM_LIMIT = 52 * 1024 * 1024


def _const_spec(shape):
    return pl.BlockSpec(shape, lambda *_: (0,) * len(shape), pipeline_mode=pl.Buffered(1))


def _params(n_axes):
    return pltpu.CompilerParams(dimension_semantics=("arbitrary",) * n_axes,
                                vmem_limit_bytes=VMEM_LIMIT)


def _rms(x, g):
    return x * lax.rsqrt(jnp.mean(x * x, axis=-1, keepdims=True) + NORM_EPS) * g


PROMPT_TM = 512
TILES_PER_SEQ = SEQ // PROMPT_TM
CONV_HALO = 32
CONV_PITCH = 4
CONV_GROUP = CONV_PITCH * SUBLANES
CONV_UNROLL = 1


def _projector(x_ref, g_ref, w_ref, cos_ref, sa_ref, sb_ref):
    hb = _rms(x_ref[...], g_ref[...]).astype(BF16)
    ch = GROUP_WIDTH

    def mm(c0):
        return jnp.dot(hb, w_ref[:, c0:c0 + ch], preferred_element_type=F32)

    def glu(c):
        return mm(c) * jax.nn.sigmoid(mm(C_CONV + c))

    def gates(gate_ref, lo=0, hi=2 * D_MODEL // ch):
        for c in range(lo * ch, hi * ch, ch):
            gate_ref[:, c:c + ch] = jax.nn.sigmoid(mm(GATE_OFF + c)).astype(BF16)

    def rope(z):
        return (z * cos_ref[...] + pltpu.roll(z, ch - ROT_DIM // 2, 1) * sa_ref[...]
                + pltpu.roll(z, ROT_DIM // 2, 1) * sb_ref[...])

    def qkv(g):
        c = g * ch
        return rope(mm(Q_OFF + c)) * (HEAD_DIM ** -0.5), rope(mm(K_OFF + c)), mm(V_OFF + c)

    return qkv, glu, gates


def _conv_slab(win_ref, lc, wdw_ref, bdw_ref, c_ref, n_rows):
    first = CONV_HALO - (CONV_WIDTH - 1)
    cols = slice(lc * LANES, (lc + 1) * LANES)
    n_acc = CONV_UNROLL * CONV_PITCH
    offs = [(i // CONV_PITCH) * CONV_GROUP + i % CONV_PITCH for i in range(n_acc)]
    bias = jnp.broadcast_to(bdw_ref[:, cols], (SUBLANES, LANES))
    for base in range(0, n_rows, CONV_UNROLL * CONV_GROUP):
        accs = [bias] * n_acc
        for k in range(CONV_WIDTH):
            wk = wdw_ref[k:k + 1, cols]
            for i in range(n_acc):
                tap = win_ref[lc, pl.ds(base + first + k + offs[i], SUBLANES, stride=CONV_PITCH), :]
                accs[i] = accs[i] + tap * wk
        for i in range(n_acc):
            c_ref[lc, pl.ds(base + offs[i], SUBLANES, stride=CONV_PITCH), :] = accs[i]


def _in_proj_prompt_kernel(x_ref, g_ref, w_ref, cos_ref, sa_ref, sb_ref, wdw_ref, bdw_ref,
                           c_ref, q_ref, kva_ref, kvb_ref, kvc_ref, gate_ref, ta_ref, tb_ref, tc_ref, ut_ref,
                           win_ref):
    tm = PROMPT_TM
    ch = GROUP_WIDTH

    @pl.when(pl.program_id(0) == 0)
    def _():
        win_ref[:, tm:tm + CONV_HALO, :] = jnp.zeros((N_SLABS, CONV_HALO, LANES), F32)

    qkv, glu, gates = _projector(x_ref, g_ref, w_ref, cos_ref, sa_ref, sb_ref)
    first_tile = pl.program_id(0) % TILES_PER_SEQ == 0
    def head_group(g):
        kv_ref, t_ref = ((kva_ref, ta_ref), (kvb_ref, tb_ref), (kvc_ref, tc_ref))[g]
        q, k, v = qkv(g)
        q_ref[:, g * ch:(g + 1) * ch] = q
        kv_ref[:, 0:ch] = k
        kv_ref[:, ch:2 * ch] = v
        n = t_ref.shape[1]
        t_ref[0:ch, :] = k[tm - n:, :].T
        t_ref[ch:2 * ch, :] = v[tm - n:, :].T

    def glu_chunk(i):
        c = i * ch
        u = glu(c)
        ut_ref[:, c:c + ch] = u[tm - CONV_HALO:, :]
        for j in range(ch // LANES):
            lc = c // LANES + j
            win_ref[lc, 0:CONV_HALO, :] = jnp.where(first_tile, 0.0, win_ref[lc, tm:tm + CONV_HALO, :])
            win_ref[lc, CONV_HALO:CONV_HALO + tm, :] = u[:, j * LANES:(j + 1) * LANES]

    n_gate = 2 * D_MODEL // ch
    after_slab = [
        lambda: glu_chunk(1), lambda: head_group(0), lambda: glu_chunk(2), lambda: head_group(1),
        lambda: glu_chunk(3), lambda: head_group(2),
        lambda: gates(gate_ref, 0, n_gate // 2), lambda: gates(gate_ref, n_gate // 2, n_gate)]
    glu_chunk(0)
    for lc in range(N_SLABS):
        _conv_slab(win_ref, lc, wdw_ref, bdw_ref, c_ref, tm)
        after_slab[lc]()


def _in_proj_prompt(x2d, g, w, cos, sa, sb, wdw, bdw):
    tm = PROMPT_TM
    rows = x2d.shape[0]
    assert GROUPS[1][0] == tm and GROUPS[2][0] == SEQ and GROUPS[0][0] <= tm
    row = lambda n: pl.BlockSpec((tm, n), lambda i: (i, 0))
    tab = pl.BlockSpec((tm, GROUP_WIDTH), lambda i: (i % TILES_PER_SEQ, 0))
    kv_w = 2 * GROUP_WIDTH
    tail = lambda n: pl.BlockSpec((None, kv_w, n), lambda i: (i // TILES_PER_SEQ, 0, 0))
    return pl.pallas_call(
        _in_proj_prompt_kernel,
        grid=(rows // tm,),
        in_specs=[row(D_MODEL), _const_spec((1, D_MODEL)), _const_spec((D_MODEL, IN_COLS)), tab, tab, tab,
                  _const_spec((CONV_WIDTH, C_CONV)), _const_spec((1, C_CONV))],
        out_specs=[pl.BlockSpec((N_SLABS, tm, LANES), lambda i: (0, i, 0)),
                   row(ATTN_WIDTH), row(kv_w), row(kv_w), row(kv_w), row(2 * D_MODEL),
                   tail(GROUPS[0][0]), tail(GROUPS[1][0]),
                   pl.BlockSpec((None, kv_w, tm), lambda i: (i // TILES_PER_SEQ, 0, i % TILES_PER_SEQ)),
                   pl.BlockSpec((None, CONV_HALO, C_CONV), lambda i: (i // TILES_PER_SEQ, 0, 0))],
        out_shape=[jax.ShapeDtypeStruct((N_SLABS, rows, LANES), F32),
                   jax.ShapeDtypeStruct((rows, ATTN_WIDTH), F32),
                   jax.ShapeDtypeStruct((rows, kv_w), F32),
                   jax.ShapeDtypeStruct((rows, kv_w), F32),
                   jax.ShapeDtypeStruct((rows, kv_w), F32),
                   jax.ShapeDtypeStruct((rows, 2 * D_MODEL), BF16),
                   jax.ShapeDtypeStruct((BATCH, kv_w, GROUPS[0][0]), F32),
                   jax.ShapeDtypeStruct((BATCH, kv_w, GROUPS[1][0]), F32),
                   jax.ShapeDtypeStruct((BATCH, kv_w, SEQ), F32),
                   jax.ShapeDtypeStruct((BATCH, CONV_HALO, C_CONV), F32)],
        scratch_shapes=[pltpu.VMEM((N_SLABS, CONV_HALO + tm, LANES), F32)],
        compiler_params=_params(1),
        name="in_proj_prompt",
    )(x2d, g, w, cos, sa, sb, wdw, bdw)


def _in_proj_sample_kernel(x_ref, g_ref, w_ref, cos_ref, sa_ref, sb_ref,
                           u_ref, qt_ref, ta_ref, tb_ref, tc_ref, gate_ref):
    qkv, glu, gates = _projector(x_ref, g_ref, w_ref, cos_ref, sa_ref, sb_ref)
    ch = GROUP_WIDTH
    for g, t_ref in enumerate((ta_ref, tb_ref, tc_ref)):
        q, k, v = qkv(g)
        qt_ref[g * ch:(g + 1) * ch, :] = q.T
        t_ref[0:ch, :] = k.T
        t_ref[ch:2 * ch, :] = v.T
    for c in range(0, C_CONV, ch):
        u = glu(c)
        for j in range(ch // LANES):
            u_ref[c // LANES + j] = u[:, j * LANES:(j + 1) * LANES]
    gates(gate_ref)


def _in_proj_sample(x2d, g, w, cos, sa, sb):
    n = x2d.shape[0]
    full = lambda r, c: pl.BlockSpec((r, c), lambda i: (0, 0))
    kv_w = 2 * GROUP_WIDTH
    return pl.pallas_call(
        _in_proj_sample_kernel,
        grid=(1,),
        in_specs=[full(n, D_MODEL), _const_spec((1, D_MODEL)), _const_spec((D_MODEL, IN_COLS)),
                  full(n, GROUP_WIDTH), full(n, GROUP_WIDTH), full(n, GROUP_WIDTH)],
        out_specs=[pl.BlockSpec((N_SLABS, n, LANES), lambda i: (0, 0, 0)),
                   full(ATTN_WIDTH, n), full(kv_w, n), full(kv_w, n), full(kv_w, n), full(n, 2 * D_MODEL)],
        out_shape=[jax.ShapeDtypeStruct((N_SLABS, n, LANES), F32),
                   jax.ShapeDtypeStruct((ATTN_WIDTH, n), F32),
                   jax.ShapeDtypeStruct((kv_w, n), F32),
                   jax.ShapeDtypeStruct((kv_w, n), F32),
                   jax.ShapeDtypeStruct((kv_w, n), F32),
                   jax.ShapeDtypeStruct((n, 2 * D_MODEL), BF16)],
        compiler_params=_params(1),
        name="in_proj_sample",
    )(x2d, g, w, cos, sa, sb)


ATTN_UNROLL = 8


def _combine_groups(lses, outs):
    mx = jnp.maximum(jnp.maximum(lses[0], lses[1]), lses[2])
    es = [jnp.exp(l - mx) for l in lses]
    num = es[0] * outs[0] + es[1] * outs[1] + es[2] * outs[2]
    return num / (es[0] + es[1] + es[2])


def _largest_divisor(n, cap):
    return max(d for d in range(1, cap + 1) if n % d == 0)


def _attn_kernel(qa_ref, qb_ref, qc_ref, ka_ref, va_ref, kb_ref, vb_ref, kc_ref, vc_ref,
                 o_ref, og_ref, lg_ref):
    lane = lax.broadcasted_iota(jnp.int32, (NK, LANES), 1)
    qi = lax.broadcasted_iota(jnp.int32, (NK, LANES), 0)
    first_head = lane < HEAD_DIM
    own_mask = lane <= qi
    prev_mask = lane >= qi
    nt = (((1,), (1,)), ((), ()))

    def blocks(g, q_ref, k_ref, v_ref, row_list):
        scores, values = [], []
        for rows, prev_rows in row_list:
            qb = q_ref[rows, :]
            q2 = jnp.concatenate([jnp.where(first_head, qb, 0.0), jnp.where(first_head, 0.0, qb)],
                                 axis=0).astype(BF16)
            keys = k_ref[rows, :].astype(BF16)
            vals = v_ref[rows, :].astype(BF16)
            mask = own_mask
            if prev_rows is not None:
                keys = jnp.concatenate([keys, k_ref[prev_rows, :].astype(BF16)], axis=0)
                vals = jnp.concatenate([vals, v_ref[prev_rows, :].astype(BF16)], axis=0)
                mask = jnp.concatenate([own_mask, prev_mask], axis=1)
            values.append(jnp.concatenate([vals, jnp.ones_like(vals)], axis=1))
            s = lax.dot_general(q2, keys, nt, preferred_element_type=F32)
            scores.append(jnp.where(jnp.concatenate([mask, mask], axis=0), s, MASKED))
        probs, maxes = [], []
        for s in scores:
            m = jnp.max(s, axis=-1, keepdims=True)
            probs.append(jnp.exp(s - m).astype(BF16))
            maxes.append(m)
        for b, (rows, _) in enumerate(row_list):
            r = jnp.dot(probs[b], values[b], preferred_element_type=F32)
            l = r[:, LANES:]
            out = r[:, :LANES] / l
            lse = maxes[b] + jnp.log(l)
            og_ref[g, rows, :] = jnp.where(first_head, out[:NK], out[NK:])
            lg_ref[g, rows, :] = jnp.where(first_head, lse[:NK], lse[NK:])

    for g, (q_ref, k_ref, v_ref) in enumerate(((qa_ref, ka_ref, va_ref), (qb_ref, kb_ref, vb_ref),
                                               (qc_ref, kc_ref, vc_ref))):
        dil = GROUPS[g][1]
        span = NK * dil
        n_later = dil * (SEQ // span - 1)

        def rows_at(start, dil=dil):
            return pl.ds(start, NK) if dil == 1 else pl.ds(start, NK, stride=dil)

        n_first = _largest_divisor(dil, ATTN_UNROLL)

        def first(t, carry, g=g, q_ref=q_ref, k_ref=k_ref, v_ref=v_ref, rows_at=rows_at, n=n_first):
            blocks(g, q_ref, k_ref, v_ref, [(rows_at(t * n + j), None) for j in range(n)])
            return carry

        lax.fori_loop(0, dil // n_first, first, 0)
        if n_later:
            n_per = _largest_divisor(n_later, ATTN_UNROLL)

            def later(t, carry, g=g, q_ref=q_ref, k_ref=k_ref, v_ref=v_ref, rows_at=rows_at, dil=dil,
                      span=span, n=n_per):
                row_list = []
                for j in range(n):
                    i = t * n + j
                    start = i % dil + (1 + i // dil) * span
                    row_list.append((rows_at(start), rows_at(start - span)))
                blocks(g, q_ref, k_ref, v_ref, row_list)
                return carry

            lax.fori_loop(0, n_later // n_per, later, 0)

    tc = 256
    for t in range(0, SEQ, tc):
        rows = slice(t, t + tc)
        o_ref[rows, :] = _combine_groups([lg_ref[g, rows, :] for g in range(N_GROUPS)],
                                         [og_ref[g, rows, :] for g in range(N_GROUPS)]).astype(o_ref.dtype)


def _attention(q3, kva3, kvb3, kvc3):
    def col(c):
        return pl.BlockSpec((None, SEQ, LANES), lambda b, hp, c=c: (b, 0, c + hp))
    n_pairs = GROUP_WIDTH // LANES
    return pl.pallas_call(
        _attn_kernel,
        grid=(BATCH, n_pairs),
        in_specs=[col(0), col(n_pairs), col(2 * n_pairs),
                  col(0), col(n_pairs), col(0), col(n_pairs), col(0), col(n_pairs)],
        out_specs=pl.BlockSpec((None, SEQ, LANES), lambda b, hp: (b, 0, hp)),
        out_shape=jax.ShapeDtypeStruct((BATCH, SEQ, GROUP_WIDTH), BF16),
        scratch_shapes=[pltpu.VMEM((N_GROUPS, SEQ, LANES), F32), pltpu.VMEM((N_GROUPS, SEQ, LANES), F32)],
        compiler_params=_params(2),
        name="prompt_attention",
    )(q3, q3, q3, kva3, kva3, kvb3, kvb3, kvc3, kvc3)


def _merge(c, x, ob, gate_ref, lng_ref, lnb_ref, wco_ref, wao_ref, wo_ref):
    mu = jnp.mean(c, axis=-1, keepdims=True)
    d = c - mu
    var = jnp.mean(d * d, axis=-1, keepdims=True)
    y = d * lax.rsqrt(var + NORM_EPS) * lng_ref[...] + lnb_ref[...]
    a_out = jnp.dot(jax.nn.silu(y).astype(BF16), wco_ref[...], preferred_element_type=F32)
    b_out = jnp.dot(ob, wao_ref[...], preferred_element_type=F32)
    merged = (gate_ref[:, 0:D_MODEL].astype(F32) * a_out
              + gate_ref[:, D_MODEL:2 * D_MODEL].astype(F32) * b_out)
    return x + jnp.dot(merged.astype(BF16), wo_ref[...], preferred_element_type=F32)


MIX_TM = 1024


def _prompt_mix_kernel(c_ref, x_ref, o_ref, gate_ref, lng_ref, lnb_ref, wco_ref, wao_ref, wo_ref, x1_ref):
    c = jnp.concatenate([c_ref[lc] for lc in range(N_SLABS)], axis=-1)
    x1_ref[...] = _merge(c, x_ref[...], o_ref[...], gate_ref, lng_ref, lnb_ref, wco_ref, wao_ref, wo_ref)


def _prompt_mix(c_slabs, x2d, o2d, gate2d, lng, lnb, wco, wao, wo):
    tm = MIX_TM
    rows = x2d.shape[0]
    row = lambda n: pl.BlockSpec((tm, n), lambda i: (i, 0))
    return pl.pallas_call(
        _prompt_mix_kernel,
        grid=(rows // tm,),
        in_specs=[pl.BlockSpec((N_SLABS, tm, LANES), lambda i: (0, i, 0)),
                  row(D_MODEL), row(GROUP_WIDTH), row(2 * D_MODEL),
                  _const_spec((1, C_CONV)), _const_spec((1, C_CONV)),
                  _const_spec((C_CONV, D_MODEL)), _const_spec((GROUP_WIDTH, D_MODEL)),
                  _const_spec((D_MODEL, D_MODEL))],
        out_specs=row(D_MODEL),
        out_shape=jax.ShapeDtypeStruct((rows, D_MODEL), F32),
        compiler_params=_params(1),
        name="prompt_mix",
    )(c_slabs, x2d, o2d, gate2d, lng, lnb, wco, wao, wo)


def _sample_mix_kernel(c_ref, x_ref, oa_ref, ob_ref, oc_ref, la_ref, lb_ref, lc_ref, gate_ref,
                       lng_ref, lnb_ref, wco_ref, wao_ref, wo_ref, x1_ref):
    ob = _combine_groups([r[...].T for r in (la_ref, lb_ref, lc_ref)],
                         [r[...].T for r in (oa_ref, ob_ref, oc_ref)]).astype(BF16)
    x1_ref[...] = _merge(c_ref[...], x_ref[...], ob, gate_ref, lng_ref, lnb_ref, wco_ref, wao_ref, wo_ref)


def _sample_mix(c, x, ogs, lgs, gate, lng, lnb, wco, wao, wo):
    rows = x.shape[0]
    full = lambda n: _const_spec((rows, n))
    per_group = _const_spec((GROUP_WIDTH, rows))
    return pl.pallas_call(
        _sample_mix_kernel,
        grid=(1,),
        in_specs=[full(C_CONV), full(D_MODEL)] + [per_group] * (2 * N_GROUPS) + [full(2 * D_MODEL),
                  _const_spec((1, C_CONV)), _const_spec((1, C_CONV)),
                  _const_spec((C_CONV, D_MODEL)), _const_spec((GROUP_WIDTH, D_MODEL)),
                  _const_spec((D_MODEL, D_MODEL))],
        out_specs=pl.BlockSpec((rows, D_MODEL), lambda i: (0, 0)),
        out_shape=jax.ShapeDtypeStruct((rows, D_MODEL), F32),
        compiler_params=_params(1),
        name="sample_mix",
    )(c, x, *ogs, *lgs, gate, lng, lnb, wco, wao, wo)


FF_CHUNK = 256


def _ffn_chunks(hb, wfi_ref, wfo_ref, acc_ref, lo, hi):
    acts = []
    for c in range(lo * FF_CHUNK, hi * FF_CHUNK, FF_CHUNK):
        gch = jnp.dot(hb, wfi_ref[:, c:c + FF_CHUNK], preferred_element_type=F32)
        uch = jnp.dot(hb, wfi_ref[:, D_FF + c:D_FF + c + FF_CHUNK], preferred_element_type=F32)
        acts.append((jax.nn.silu(gch) * uch).astype(BF16))
    act = jnp.concatenate(acts, axis=1) if len(acts) > 1 else acts[0]
    acc_ref[...] += jnp.dot(act, wfo_ref[lo * FF_CHUNK:hi * FF_CHUNK, :], preferred_element_type=F32)


def _ffn_finish(x1, ffn_out, p_ref, gp_ref, wpg_ref, wpp_ref, gfin_ref, y_ref):
    x2 = x1 + ffn_out
    hp = _rms(x2, gp_ref[...]).astype(BF16)
    gate = jax.nn.sigmoid(jnp.dot(hp, wpg_ref[...], preferred_element_type=F32))
    pe = jnp.dot(p_ref[...].astype(BF16), wpp_ref[...], preferred_element_type=F32)
    x3 = x2 + gate * pe
    y_ref[...] = _rms(x3, gfin_ref[...])


def _ffn_kernel(x_ref, p_ref, gf_ref, wfi_ref, wfo_ref, gp_ref, wpg_ref, wpp_ref, gfin_ref, y_ref, acc_ref):
    x1 = x_ref[...]
    hb = _rms(x1, gf_ref[...]).astype(BF16)
    acc_ref[...] = jnp.zeros_like(acc_ref)
    _ffn_chunks(hb, wfi_ref, wfo_ref, acc_ref, 0, D_FF // FF_CHUNK)
    _ffn_finish(x1, acc_ref[...], p_ref, gp_ref, wpg_ref, wpp_ref, gfin_ref, y_ref)


def _ffn(x1, p, gf, wfi, wfo, gp, wpg, wpp, gfin, tm):
    rows = x1.shape[0]
    row = lambda n: pl.BlockSpec((tm, n), lambda i: (i, 0))
    return pl.pallas_call(
        _ffn_kernel,
        grid=(rows // tm,),
        in_specs=[row(D_MODEL), row(PLE_DIM), _const_spec((1, D_MODEL)),
                  _const_spec((D_MODEL, 2 * D_FF)), _const_spec((D_FF, D_MODEL)),
                  _const_spec((1, D_MODEL)), _const_spec((D_MODEL, D_MODEL)),
                  _const_spec((PLE_DIM, D_MODEL)), _const_spec((1, D_MODEL))],
        out_specs=row(D_MODEL),
        out_shape=jax.ShapeDtypeStruct((rows, D_MODEL), F32),
        scratch_shapes=[pltpu.VMEM((tm, D_MODEL), F32)],
        compiler_params=_params(1),
        name="ffn_ple",
    )(x1, p, gf, wfi, wfo, gp, wpg, wpp, gfin)


SAMPLE_CONV_BLOCK = 32


def _sample_conv_kernel(state_ref, u_ref, wdw_ref, bdw_ref, c_ref, new_ref):
    ctx = CONV_WIDTH - 1
    for lc in range(N_SLABS):
        cols = slice(lc * LANES, (lc + 1) * LANES)
        u = u_ref[lc]
        acc = u * wdw_ref[ctx:ctx + 1, cols] + bdw_ref[:, cols]
        for k in range(ctx):
            acc = acc + state_ref[k, :, cols] * wdw_ref[k:k + 1, cols]
        c_ref[:, cols] = acc
        new_ref[ctx - 1, :, cols] = u
    for k in range(ctx - 1):
        new_ref[k] = state_ref[k + 1]


def _sample_conv(state_t, u_slabs, wdw, bdw):
    ctx, n, _ = state_t.shape
    sb = SAMPLE_CONV_BLOCK
    return pl.pallas_call(
        _sample_conv_kernel,
        grid=(n // sb,),
        in_specs=[pl.BlockSpec((ctx, sb, C_CONV), lambda i: (0, i, 0)),
                  pl.BlockSpec((N_SLABS, sb, LANES), lambda i: (0, i, 0)),
                  _const_spec((CONV_WIDTH, C_CONV)), _const_spec((1, C_CONV))],
        out_specs=[pl.BlockSpec((sb, C_CONV), lambda i: (i, 0)),
                   pl.BlockSpec((ctx, sb, C_CONV), lambda i: (0, i, 0))],
        out_shape=[jax.ShapeDtypeStruct((n, C_CONV), F32),
                   jax.ShapeDtypeStruct((ctx, n, C_CONV), F32)],
        compiler_params=_params(1),
        name="sample_conv",
    )(state_t, u_slabs, wdw, bdw)


def _cache_sample(q_ref, new_ref, cache_ref, out_ref, o_ref, lse_ref, s, sample, dil):
    length = cache_ref.shape[-1]
    pos = lax.broadcasted_iota(jnp.int32, (1, length), 1)
    used = (pos & (dil - 1)) == 0
    lane = lax.broadcasted_iota(jnp.int32, (HEAD_DIM, LANES), 1)
    last = lane == LANES - 1
    mine = lane == sample

    def pick(ref, r0):
        return jnp.sum(jnp.where(mine, ref[r0:r0 + HEAD_DIM, :], 0.0), axis=1, keepdims=True)

    def shift(kv, h, new):
        rolled = pltpu.roll(cache_ref[s, kv, h], length - 1, 1)
        out_ref[s, kv, h] = rolled
        out_ref[s, kv, h, :, length - LANES:] = jnp.where(last, new, rolled[:, length - LANES:])

    heads = range(H_G)
    q = [pick(q_ref, h * HEAD_DIM) for h in heads]
    k_new = [pick(new_ref, h * HEAD_DIM) for h in heads]
    v_new = [pick(new_ref, GROUP_WIDTH + h * HEAD_DIM) for h in heads]
    shift(0, 0, k_new[0])
    shift(1, 0, v_new[0])
    sc = [jnp.where(used, jnp.sum(cache_ref[s, 0, h] * q[h], axis=0, keepdims=True), MASKED) for h in heads]
    s_new = [jnp.sum(k_new[h] * q[h], axis=0, keepdims=True) for h in heads]
    m = [jnp.maximum(jnp.max(sc[h], axis=1, keepdims=True), s_new[h]) for h in heads]
    shift(0, 1, k_new[1])
    shift(1, 1, v_new[1])
    p = [jnp.exp(sc[h] - m[h]) for h in heads]
    p_new = [jnp.exp(s_new[h] - m[h]) for h in heads]
    l = [jnp.sum(p[h], axis=1, keepdims=True) + p_new[h] for h in heads]
    shift(0, 2, k_new[2])
    shift(1, 2, v_new[2])
    o = [(jnp.sum(cache_ref[s, 1, h] * p[h], axis=1, keepdims=True) + v_new[h] * p_new[h]) / l[h]
         for h in heads]
    shift(0, 3, k_new[3])
    shift(1, 3, v_new[3])
    for h in heads:
        rows = slice(h * HEAD_DIM, (h + 1) * HEAD_DIM)
        o_ref[rows, :] = jnp.where(mine, o[h], o_ref[rows, :])
        lse_ref[rows, :] = jnp.where(mine, m[h] + jnp.log(l[h]), lse_ref[rows, :])


FFN_PARTS = 4
FUSED_VMEM_LIMIT = 57 * 1024 * 1024


def _ffn_cache_kernel(x_ref, p_ref, gf_ref, wfi_ref, wfo_ref, gp_ref, wpg_ref, wpp_ref, gfin_ref, *refs):
    ng = N_GROUPS
    q_refs, new_refs, cache_refs = refs[0:ng], refs[ng:2 * ng], refs[2 * ng:3 * ng]
    y_ref = refs[3 * ng]
    out_refs, o_refs, lse_refs = refs[3 * ng + 1:4 * ng + 1], refs[4 * ng + 1:5 * ng + 1], refs[5 * ng + 1:6 * ng + 1]
    (hb_ref,) = refs[6 * ng + 1:]
    acc_ref = y_ref
    tile = pl.program_id(0)
    part = pl.program_id(1)

    @pl.when((tile == 0) & (part == 0))
    def _():
        for r in o_refs + lse_refs:
            r[...] = jnp.zeros_like(r)

    n_chunks = D_FF // FF_CHUNK
    bounds = [(n_chunks + 2) * j // FFN_PARTS for j in range(FFN_PARTS)] + [n_chunks]

    for j in range(FFN_PARTS):
        @pl.when(part == j)
        def _(j=j):
            if j == 0:
                hb_ref[...] = _rms(x_ref[...], gf_ref[...]).astype(BF16)
                acc_ref[...] = jnp.zeros_like(acc_ref)
            _ffn_chunks(hb_ref[...], wfi_ref, wfo_ref, acc_ref, bounds[j], bounds[j + 1])
            if j == FFN_PARTS - 1:
                _ffn_finish(x_ref[...], acc_ref[...], p_ref, gp_ref, wpg_ref, wpp_ref, gfin_ref, y_ref)
            for g in reversed(range(ng)):
                _cache_sample(q_refs[g], new_refs[g], cache_refs[g], out_refs[g], o_refs[g], lse_refs[g],
                              0, tile * FFN_PARTS + part, GROUPS[g][1])


def _ffn_cache(x1, p, gf, wfi, wfo, gp, wpg, wpp, gfin, q_t, news, caches):
    tm = PROMPT_TM
    rows = x1.shape[0]
    n = caches[0].shape[0]
    assert (rows // tm) * FFN_PARTS == n
    row = lambda w: pl.BlockSpec((tm, w), lambda i, k: (i, 0))
    bigs = [pl.BlockSpec((1,) + c.shape[1:], lambda i, k: (i * FFN_PARTS + k, 0, 0, 0, 0)) for c in caches]
    res = pl.BlockSpec((GROUP_WIDTH, n), lambda i, k: (0, 0))
    res_shape = jax.ShapeDtypeStruct((GROUP_WIDTH, n), F32)
    ng = N_GROUPS
    return pl.pallas_call(
        _ffn_cache_kernel,
        grid=(rows // tm, FFN_PARTS),
        in_specs=[row(D_MODEL), row(PLE_DIM), _const_spec((1, D_MODEL)),
                  _const_spec((D_MODEL, 2 * D_FF)), _const_spec((D_FF, D_MODEL)),
                  _const_spec((1, D_MODEL)), _const_spec((D_MODEL, D_MODEL)),
                  _const_spec((PLE_DIM, D_MODEL)), _const_spec((1, D_MODEL))]
                 + [pl.BlockSpec((GROUP_WIDTH, n), lambda i, k, g=g: (g, 0), pipeline_mode=pl.Buffered(1))
                    for g in range(ng)]
                 + [_const_spec((2 * GROUP_WIDTH, n))] * ng + bigs,
        out_specs=[row(D_MODEL)] + bigs + [res] * (2 * ng),
        out_shape=[jax.ShapeDtypeStruct((rows, D_MODEL), F32)]
                  + [jax.ShapeDtypeStruct(c.shape, F32) for c in caches] + [res_shape] * (2 * ng),
        scratch_shapes=[pltpu.VMEM((tm, D_MODEL), BF16)],
        compiler_params=pltpu.CompilerParams(dimension_semantics=("arbitrary", "arbitrary"),
                                             vmem_limit_bytes=FUSED_VMEM_LIMIT),
        name="ffn_ple_caches",
    )(x1, p, gf, wfi, wfo, gp, wpg, wpp, gfin, *([q_t] * ng), *news, *caches)


def _rope_tables(pos):
    half = ROT_DIM // 2
    inv_freq = ROPE_THETA ** (-jnp.arange(half, dtype=F32) / half)
    ang = pos.astype(F32)[:, None] * inv_freq[None, :]
    cos, sin = jnp.cos(ang), jnp.sin(ang)
    n = pos.shape[0]
    zeros = lambda w: jnp.zeros((n, w), F32)
    cos_h = jnp.concatenate([cos, cos, jnp.ones((n, HEAD_DIM - ROT_DIM), F32)], axis=-1)
    sa_h = jnp.concatenate([-sin, zeros(HEAD_DIM - half)], axis=-1)
    sb_h = jnp.concatenate([zeros(half), sin, zeros(HEAD_DIM - ROT_DIM)], axis=-1)
    return tuple(jnp.tile(t, (1, H_G)) for t in (cos_h, sa_h, sb_h))


def _position_major(t):
    b, _, length = t.shape
    return jnp.transpose(t.reshape(b, 2, H_G, HEAD_DIM, length), (0, 4, 1, 2, 3))[None]


def kernel(x_prompt, x_sample, state_conv, cache_win_a, cache_win_b, cache_win_c, p_prompt, p_sample,
           w_in, g_mix, w_dw, b_dw, ln_g, ln_b, w_conv_out, w_attn_out, w_o, g_ffn, w_ffn_in, w_ffn_out,
           g_ple, w_ple_gate, w_ple_proj, g_final):
    assert w_in.shape[0] == 1, "single-layer step"
    w_in_b = w_in[0].astype(BF16)
    wco = w_conv_out[0].astype(BF16)
    wao = w_attn_out[0].astype(BF16)
    wo = w_o[0].astype(BF16)
    wfi = w_ffn_in[0].astype(BF16)
    wfo = w_ffn_out[0].astype(BF16)
    wpg = w_ple_gate[0].astype(BF16)
    wpp = w_ple_proj[0].astype(BF16)
    gfin = g_final.reshape(1, D_MODEL)
    n_prompt = BATCH * SEQ
    ctx = CONV_WIDTH - 1

    xp = x_prompt.reshape(n_prompt, D_MODEL)
    tabs_p = _rope_tables(jnp.arange(SEQ, dtype=jnp.int32))
    c_slabs, q, kva, kvb, kvc, gates, tail_a, tail_b, tail_c, u_tail = _in_proj_prompt(
        xp, g_mix, w_in_b, *tabs_p, w_dw[0], b_dw)
    per_batch = lambda t: t.reshape(BATCH, SEQ, t.shape[-1])
    o = _attention(per_batch(q), per_batch(kva), per_batch(kvb), per_batch(kvc))
    x1 = _prompt_mix(c_slabs, xp, o.reshape(n_prompt, GROUP_WIDTH), gates, ln_g, ln_b, wco, wao, wo)
    new_conv_prompt = u_tail[None, :, CONV_HALO - ctx:, :]

    n_s = DEC_BATCH
    xs = x_sample.reshape(n_s, D_MODEL)
    tabs_s = tuple(jnp.broadcast_to(t, (n_s, GROUP_WIDTH))
                   for t in _rope_tables(jnp.full((1,), PAST_LEN, jnp.int32)))
    u_s, q_t, new_a, new_b, new_c, gates_s = _in_proj_sample(xs, g_mix, w_in_b, *tabs_s)
    c_s, new_conv_t = _sample_conv(jnp.transpose(state_conv[0], (1, 0, 2)), u_s, w_dw[0], b_dw)
    caches_t = [jnp.transpose(c[0], (0, 2, 3, 4, 1)) for c in (cache_win_a, cache_win_b, cache_win_c)]
    fused = _ffn_cache(x1, p_prompt.reshape(n_prompt, PLE_DIM), g_ffn, wfi, wfo, g_ple, wpg, wpp, gfin,
                       q_t, [new_a, new_b, new_c], caches_t)
    y_prompt = fused[0]
    new_wins = [jnp.transpose(t, (0, 4, 1, 2, 3))[None] for t in fused[1:1 + N_GROUPS]]
    ogs = fused[1 + N_GROUPS:1 + 2 * N_GROUPS]
    lgs = fused[1 + 2 * N_GROUPS:]
    x1_s = _sample_mix(c_s, xs, ogs, lgs, gates_s, ln_g, ln_b, wco, wao, wo)
    y_sample = _ffn(x1_s, p_sample.reshape(n_s, PLE_DIM), g_ffn, wfi, wfo, g_ple, wpg, wpp, gfin,
                    n_s).reshape(n_s, 1, D_MODEL)
    new_conv_sample = jnp.transpose(new_conv_t, (1, 0, 2))[None]

    return (y_prompt.reshape(BATCH, SEQ, D_MODEL), y_sample, new_conv_prompt, _position_major(tail_a), _position_major(tail_b),
            _position_major(tail_c), new_conv_sample, new_wins[0], new_wins[1], new_wins[2])
```

```python
import jax
import jax.numpy as jnp
from jax import lax
from jax.experimental import pallas as pl
from jax.experimental.pallas import tpu as pltpu

D_MODEL = 1024
BATCH = 8
SEQ = 2048
DEC_BATCH = 128
PAST_LEN = 8192
HEAD_DIM = 64
GROUPS = ((128, 1), (512, 4), (2048, 16))
H_G = 4
N_GROUPS = len(GROUPS)
GROUP_WIDTH = H_G * HEAD_DIM
ATTN_WIDTH = N_GROUPS * GROUP_WIDTH
ROT_DIM = HEAD_DIM // 4
ROPE_THETA = 500000.0
C_CONV = D_MODEL
CONV_WIDTH = 31
D_FF = 2816
PLE_DIM = 256
NORM_EPS = 1e-6
NK = 128
Q_OFF = 2 * C_CONV
K_OFF = Q_OFF + ATTN_WIDTH
V_OFF = K_OFF + ATTN_WIDTH
GATE_OFF = V_OFF + ATTN_WIDTH
IN_COLS = GATE_OFF + 2 * D_MODEL

F32 = jnp.float32
BF16 = jnp.bfloat16
LANES = 128
SUBLANES = 8
N_SLABS = C_CONV // LANES
MASKED = -jnp.inf
VMEM_LIMIT = 52 * 1024 * 1024


def _const_spec(shape):
    return pl.BlockSpec(shape, lambda *_: (0,) * len(shape), pipeline_mode=pl.Buffered(1))


def _params(n_axes):
    return pltpu.CompilerParams(dimension_semantics=("arbitrary",) * n_axes,
                                vmem_limit_bytes=VMEM_LIMIT)


def _rms(x, g):
    return x * lax.rsqrt(jnp.mean(x * x, axis=-1, keepdims=True) + NORM_EPS) * g


def _sigmoid(x):
    return 0.5 * jnp.tanh(0.5 * x) + 0.5


PROMPT_TM = 512
TILES_PER_SEQ = SEQ // PROMPT_TM
CONV_HALO = 32
CONV_PITCH = 4
CONV_GROUP = CONV_PITCH * SUBLANES
CONV_UNROLL = 1


def _projector(x_ref, g_ref, w_ref, cos_ref, sa_ref, sb_ref):
    hb = _rms(x_ref[...], g_ref[...]).astype(BF16)
    ch = GROUP_WIDTH

    def mm(c0):
        return jnp.dot(hb, w_ref[:, c0:c0 + ch], preferred_element_type=F32)

    def glu(c):
        return mm(c) * _sigmoid(mm(C_CONV + c))

    def gates(gate_ref, lo=0, hi=2 * D_MODEL // ch):
        for c in range(lo * ch, hi * ch, ch):
            gate_ref[:, c:c + ch] = _sigmoid(mm(GATE_OFF + c)).astype(BF16)

    def rope(z):
        return (z * cos_ref[...] + pltpu.roll(z, ch - ROT_DIM // 2, 1) * sa_ref[...]
                + pltpu.roll(z, ROT_DIM // 2, 1) * sb_ref[...])

    def qkv(g):
        c = g * ch
        return rope(mm(Q_OFF + c)) * (HEAD_DIM ** -0.5), rope(mm(K_OFF + c)), mm(V_OFF + c)

    return qkv, glu, gates


def _conv_slab(win_ref, lc, wdw_ref, bdw_ref, c_ref, n_rows):
    first = CONV_HALO - (CONV_WIDTH - 1)
    cols = slice(lc * LANES, (lc + 1) * LANES)
    n_acc = CONV_UNROLL * CONV_PITCH
    offs = [(i // CONV_PITCH) * CONV_GROUP + i % CONV_PITCH for i in range(n_acc)]
    bias = jnp.broadcast_to(bdw_ref[:, cols], (SUBLANES, LANES))
    for base in range(0, n_rows, CONV_UNROLL * CONV_GROUP):
        accs = [bias] * n_acc
        for k in range(CONV_WIDTH):
            wk = wdw_ref[k:k + 1, cols]
            for i in range(n_acc):
                tap = win_ref[lc, pl.ds(base + first + k + offs[i], SUBLANES, stride=CONV_PITCH), :]
                accs[i] = accs[i] + tap * wk
        for i in range(n_acc):
            c_ref[lc, pl.ds(base + offs[i], SUBLANES, stride=CONV_PITCH), :] = accs[i]


def _in_proj_prompt_kernel(x_ref, g_ref, w_ref, cos_ref, sa_ref, sb_ref, wdw_ref, bdw_ref,
                           c_ref, q_ref, kva_ref, kvb_ref, kvc_ref, gate_ref, ta_ref, tb_ref, tc_ref, ut_ref,
                           win_ref):
    tm = PROMPT_TM
    ch = GROUP_WIDTH

    @pl.when(pl.program_id(0) == 0)
    def _():
        win_ref[:, tm:tm + CONV_HALO, :] = jnp.zeros((N_SLABS, CONV_HALO, LANES), F32)

    qkv, glu, gates = _projector(x_ref, g_ref, w_ref, cos_ref, sa_ref, sb_ref)
    first_tile = pl.program_id(0) % TILES_PER_SEQ == 0
    def head_group(g):
        kv_ref, t_ref = ((kva_ref, ta_ref), (kvb_ref, tb_ref), (kvc_ref, tc_ref))[g]
        q, k, v = qkv(g)
        q_ref[:, g * ch:(g + 1) * ch] = q
        kv_ref[:, 0:ch] = k
        kv_ref[:, ch:2 * ch] = v
        n = t_ref.shape[1]
        t_ref[0:ch, :] = k[tm - n:, :].T
        t_ref[ch:2 * ch, :] = v[tm - n:, :].T

    def glu_chunk(i):
        c = i * ch
        u = glu(c)
        ut_ref[:, c:c + ch] = u[tm - CONV_HALO:, :]
        for j in range(ch // LANES):
            lc = c // LANES + j
            win_ref[lc, 0:CONV_HALO, :] = jnp.where(first_tile, 0.0, win_ref[lc, tm:tm + CONV_HALO, :])
            win_ref[lc, CONV_HALO:CONV_HALO + tm, :] = u[:, j * LANES:(j + 1) * LANES]

    n_gate = 2 * D_MODEL // ch
    after_slab = [
        lambda: glu_chunk(1), lambda: head_group(0), lambda: glu_chunk(2), lambda: head_group(1),
        lambda: glu_chunk(3), lambda: head_group(2),
        lambda: gates(gate_ref, 0, n_gate // 2), lambda: gates(gate_ref, n_gate // 2, n_gate)]
    glu_chunk(0)
    for lc in range(N_SLABS):
        _conv_slab(win_ref, lc, wdw_ref, bdw_ref, c_ref, tm)
        after_slab[lc]()


def _in_proj_prompt(x2d, g, w, cos, sa, sb, wdw, bdw):
    tm = PROMPT_TM
    rows = x2d.shape[0]
    assert GROUPS[1][0] == tm and GROUPS[2][0] == SEQ and GROUPS[0][0] <= tm
    row = lambda n: pl.BlockSpec((tm, n), lambda i: (i, 0))
    tab = pl.BlockSpec((tm, GROUP_WIDTH), lambda i: (i % TILES_PER_SEQ, 0))
    kv_w = 2 * GROUP_WIDTH
    tail = lambda n: pl.BlockSpec((None, kv_w, n), lambda i: (i // TILES_PER_SEQ, 0, 0))
    return pl.pallas_call(
        _in_proj_prompt_kernel,
        grid=(rows // tm,),
        in_specs=[row(D_MODEL), _const_spec((1, D_MODEL)), _const_spec((D_MODEL, IN_COLS)), tab, tab, tab,
                  _const_spec((CONV_WIDTH, C_CONV)), _const_spec((1, C_CONV))],
        out_specs=[pl.BlockSpec((N_SLABS, tm, LANES), lambda i: (0, i, 0)),
                   row(ATTN_WIDTH), row(kv_w), row(kv_w), row(kv_w), row(2 * D_MODEL),
                   tail(GROUPS[0][0]), tail(GROUPS[1][0]),
                   pl.BlockSpec((None, kv_w, tm), lambda i: (i // TILES_PER_SEQ, 0, i % TILES_PER_SEQ)),
                   pl.BlockSpec((None, CONV_HALO, C_CONV), lambda i: (i // TILES_PER_SEQ, 0, 0))],
        out_shape=[jax.ShapeDtypeStruct((N_SLABS, rows, LANES), F32),
                   jax.ShapeDtypeStruct((rows, ATTN_WIDTH), F32),
                   jax.ShapeDtypeStruct((rows, kv_w), F32),
                   jax.ShapeDtypeStruct((rows, kv_w), F32),
                   jax.ShapeDtypeStruct((rows, kv_w), F32),
                   jax.ShapeDtypeStruct((rows, 2 * D_MODEL), BF16),
                   jax.ShapeDtypeStruct((BATCH, kv_w, GROUPS[0][0]), F32),
                   jax.ShapeDtypeStruct((BATCH, kv_w, GROUPS[1][0]), F32),
                   jax.ShapeDtypeStruct((BATCH, kv_w, SEQ), F32),
                   jax.ShapeDtypeStruct((BATCH, CONV_HALO, C_CONV), F32)],
        scratch_shapes=[pltpu.VMEM((N_SLABS, CONV_HALO + tm, LANES), F32)],
        compiler_params=_params(1),
        name="in_proj_prompt",
    )(x2d, g, w, cos, sa, sb, wdw, bdw)


def _in_proj_sample_kernel(x_ref, g_ref, w_ref, cos_ref, sa_ref, sb_ref,
                           u_ref, qt_ref, ta_ref, tb_ref, tc_ref, gate_ref):
    qkv, glu, gates = _projector(x_ref, g_ref, w_ref, cos_ref, sa_ref, sb_ref)
    ch = GROUP_WIDTH
    for g, t_ref in enumerate((ta_ref, tb_ref, tc_ref)):
        q, k, v = qkv(g)
        qt_ref[g * ch:(g + 1) * ch, :] = q.T
        t_ref[0:ch, :] = k.T
        t_ref[ch:2 * ch, :] = v.T
    for c in range(0, C_CONV, ch):
        u = glu(c)
        for j in range(ch // LANES):
            u_ref[c // LANES + j] = u[:, j * LANES:(j + 1) * LANES]
    gates(gate_ref)


def _in_proj_sample(x2d, g, w, cos, sa, sb):
    n = x2d.shape[0]
    full = lambda r, c: pl.BlockSpec((r, c), lambda i: (0, 0))
    kv_w = 2 * GROUP_WIDTH
    return pl.pallas_call(
        _in_proj_sample_kernel,
        grid=(1,),
        in_specs=[full(n, D_MODEL), _const_spec((1, D_MODEL)), _const_spec((D_MODEL, IN_COLS)),
                  full(n, GROUP_WIDTH), full(n, GROUP_WIDTH), full(n, GROUP_WIDTH)],
        out_specs=[pl.BlockSpec((N_SLABS, n, LANES), lambda i: (0, 0, 0)),
                   full(ATTN_WIDTH, n), full(kv_w, n), full(kv_w, n), full(kv_w, n), full(n, 2 * D_MODEL)],
        out_shape=[jax.ShapeDtypeStruct((N_SLABS, n, LANES), F32),
                   jax.ShapeDtypeStruct((ATTN_WIDTH, n), F32),
                   jax.ShapeDtypeStruct((kv_w, n), F32),
                   jax.ShapeDtypeStruct((kv_w, n), F32),
                   jax.ShapeDtypeStruct((kv_w, n), F32),
                   jax.ShapeDtypeStruct((n, 2 * D_MODEL), BF16)],
        compiler_params=_params(1),
        name="in_proj_sample",
    )(x2d, g, w, cos, sa, sb)


ATTN_UNROLL = 8


def _combine_groups(lses, outs):
    mx = jnp.maximum(jnp.maximum(lses[0], lses[1]), lses[2])
    es = [jnp.exp(l - mx) for l in lses]
    num = es[0] * outs[0] + es[1] * outs[1] + es[2] * outs[2]
    return num / (es[0] + es[1] + es[2])


def _largest_divisor(n, cap):
    return max(d for d in range(1, cap + 1) if n % d == 0)


def _attn_kernel(qa_ref, qb_ref, qc_ref, ka_ref, va_ref, kb_ref, vb_ref, kc_ref, vc_ref,
                 o_ref, og_ref, lg_ref):
    lane = lax.broadcasted_iota(jnp.int32, (NK, LANES), 1)
    qi = lax.broadcasted_iota(jnp.int32, (NK, LANES), 0)
    first_head = lane < HEAD_DIM
    own_mask = lane <= qi
    prev_mask = lane >= qi
    nt = (((1,), (1,)), ((), ()))

    def blocks(g, q_ref, k_ref, v_ref, row_list):
        scores, values = [], []
        for rows, prev_rows in row_list:
            qb = q_ref[rows, :]
            q2 = jnp.concatenate([jnp.where(first_head, qb, 0.0), jnp.where(first_head, 0.0, qb)],
                                 axis=0).astype(BF16)
            keys = k_ref[rows, :].astype(BF16)
            vals = v_ref[rows, :].astype(BF16)
            mask = own_mask
            if prev_rows is not None:
                keys = jnp.concatenate([keys, k_ref[prev_rows, :].astype(BF16)], axis=0)
                vals = jnp.concatenate([vals, v_ref[prev_rows, :].astype(BF16)], axis=0)
                mask = jnp.concatenate([own_mask, prev_mask], axis=1)
            values.append(jnp.concatenate([vals, jnp.ones_like(vals)], axis=1))
            s = lax.dot_general(q2, keys, nt, preferred_element_type=F32)
            scores.append(jnp.where(jnp.concatenate([mask, mask], axis=0), s, MASKED))
        probs, maxes = [], []
        for s in scores:
            m = jnp.max(s, axis=-1, keepdims=True)
            probs.append(jnp.exp(s - m).astype(BF16))
            maxes.append(m)
        for b, (rows, _) in enumerate(row_list):
            r = jnp.dot(probs[b], values[b], preferred_element_type=F32)
            l = r[:, LANES:]
            out = r[:, :LANES] / l
            lse = maxes[b] + jnp.log(l)
            og_ref[g, rows, :] = jnp.where(first_head, out[:NK], out[NK:])
            lg_ref[g, rows, :] = jnp.where(first_head, lse[:NK], lse[NK:])

    for g, (q_ref, k_ref, v_ref) in enumerate(((qa_ref, ka_ref, va_ref), (qb_ref, kb_ref, vb_ref),
                                               (qc_ref, kc_ref, vc_ref))):
        dil = GROUPS[g][1]
        span = NK * dil
        n_later = dil * (SEQ // span - 1)

        def rows_at(start, dil=dil):
            return pl.ds(start, NK) if dil == 1 else pl.ds(start, NK, stride=dil)

        n_first = _largest_divisor(dil, ATTN_UNROLL)

        def first(t, carry, g=g, q_ref=q_ref, k_ref=k_ref, v_ref=v_ref, rows_at=rows_at, n=n_first):
            blocks(g, q_ref, k_ref, v_ref, [(rows_at(t * n + j), None) for j in range(n)])
            return carry

        lax.fori_loop(0, dil // n_first, first, 0)
        if n_later:
            n_per = _largest_divisor(n_later, ATTN_UNROLL)

            def later(t, carry, g=g, q_ref=q_ref, k_ref=k_ref, v_ref=v_ref, rows_at=rows_at, dil=dil,
                      span=span, n=n_per):
                row_list = []
                for j in range(n):
                    i = t * n + j
                    start = i % dil + (1 + i // dil) * span
                    row_list.append((rows_at(start), rows_at(start - span)))
                blocks(g, q_ref, k_ref, v_ref, row_list)
                return carry

            lax.fori_loop(0, n_later // n_per, later, 0)

    tc = 256
    for t in range(0, SEQ, tc):
        rows = slice(t, t + tc)
        o_ref[rows, :] = _combine_groups([lg_ref[g, rows, :] for g in range(N_GROUPS)],
                                         [og_ref[g, rows, :] for g in range(N_GROUPS)]).astype(o_ref.dtype)


def _attention(q3, kva3, kvb3, kvc3):
    def col(c):
        return pl.BlockSpec((None, SEQ, LANES), lambda b, hp, c=c: (b, 0, c + hp))
    n_pairs = GROUP_WIDTH // LANES
    return pl.pallas_call(
        _attn_kernel,
        grid=(BATCH, n_pairs),
        in_specs=[col(0), col(n_pairs), col(2 * n_pairs),
                  col(0), col(n_pairs), col(0), col(n_pairs), col(0), col(n_pairs)],
        out_specs=pl.BlockSpec((None, SEQ, LANES), lambda b, hp: (b, 0, hp)),
        out_shape=jax.ShapeDtypeStruct((BATCH, SEQ, GROUP_WIDTH), BF16),
        scratch_shapes=[pltpu.VMEM((N_GROUPS, SEQ, LANES), F32), pltpu.VMEM((N_GROUPS, SEQ, LANES), F32)],
        compiler_params=_params(2),
        name="prompt_attention",
    )(q3, q3, q3, kva3, kva3, kvb3, kvb3, kvc3, kvc3)


def _merge(c, x, ob, gate_ref, lng_ref, lnb_ref, wco_ref, wao_ref, wo_ref):
    mu = jnp.mean(c, axis=-1, keepdims=True)
    d = c - mu
    var = jnp.mean(d * d, axis=-1, keepdims=True)
    y = d * lax.rsqrt(var + NORM_EPS) * lng_ref[...] + lnb_ref[...]
    a_out = jnp.dot(jax.nn.silu(y).astype(BF16), wco_ref[...], preferred_element_type=F32)
    b_out = jnp.dot(ob, wao_ref[...], preferred_element_type=F32)
    merged = (gate_ref[:, 0:D_MODEL].astype(F32) * a_out
              + gate_ref[:, D_MODEL:2 * D_MODEL].astype(F32) * b_out)
    return x + jnp.dot(merged.astype(BF16), wo_ref[...], preferred_element_type=F32)


MIX_TM = 1024


def _prompt_mix_kernel(c_ref, x_ref, o_ref, gate_ref, lng_ref, lnb_ref, wco_ref, wao_ref, wo_ref, x1_ref):
    c = jnp.concatenate([c_ref[lc] for lc in range(N_SLABS)], axis=-1)
    x1_ref[...] = _merge(c, x_ref[...], o_ref[...], gate_ref, lng_ref, lnb_ref, wco_ref, wao_ref, wo_ref)


def _prompt_mix(c_slabs, x2d, o2d, gate2d, lng, lnb, wco, wao, wo):
    tm = MIX_TM
    rows = x2d.shape[0]
    row = lambda n: pl.BlockSpec((tm, n), lambda i: (i, 0))
    return pl.pallas_call(
        _prompt_mix_kernel,
        grid=(rows // tm,),
        in_specs=[pl.BlockSpec((N_SLABS, tm, LANES), lambda i: (0, i, 0)),
                  row(D_MODEL), row(GROUP_WIDTH), row(2 * D_MODEL),
                  _const_spec((1, C_CONV)), _const_spec((1, C_CONV)),
                  _const_spec((C_CONV, D_MODEL)), _const_spec((GROUP_WIDTH, D_MODEL)),
                  _const_spec((D_MODEL, D_MODEL))],
        out_specs=row(D_MODEL),
        out_shape=jax.ShapeDtypeStruct((rows, D_MODEL), F32),
        compiler_params=_params(1),
        name="prompt_mix",
    )(c_slabs, x2d, o2d, gate2d, lng, lnb, wco, wao, wo)


def _sample_mix_kernel(c_ref, x_ref, oa_ref, ob_ref, oc_ref, la_ref, lb_ref, lc_ref, gate_ref,
                       lng_ref, lnb_ref, wco_ref, wao_ref, wo_ref, x1_ref):
    ob = _combine_groups([r[...].T for r in (la_ref, lb_ref, lc_ref)],
                         [r[...].T for r in (oa_ref, ob_ref, oc_ref)]).astype(BF16)
    x1_ref[...] = _merge(c_ref[...], x_ref[...], ob, gate_ref, lng_ref, lnb_ref, wco_ref, wao_ref, wo_ref)


def _sample_mix(c, x, ogs, lgs, gate, lng, lnb, wco, wao, wo):
    rows = x.shape[0]
    full = lambda n: _const_spec((rows, n))
    per_group = _const_spec((GROUP_WIDTH, rows))
    return pl.pallas_call(
        _sample_mix_kernel,
        grid=(1,),
        in_specs=[full(C_CONV), full(D_MODEL)] + [per_group] * (2 * N_GROUPS) + [full(2 * D_MODEL),
                  _const_spec((1, C_CONV)), _const_spec((1, C_CONV)),
                  _const_spec((C_CONV, D_MODEL)), _const_spec((GROUP_WIDTH, D_MODEL)),
                  _const_spec((D_MODEL, D_MODEL))],
        out_specs=pl.BlockSpec((rows, D_MODEL), lambda i: (0, 0)),
        out_shape=jax.ShapeDtypeStruct((rows, D_MODEL), F32),
        compiler_params=_params(1),
        name="sample_mix",
    )(c, x, *ogs, *lgs, gate, lng, lnb, wco, wao, wo)


FF_CHUNK = 256


def _ffn_chunks(hb, wfi_ref, wfo_ref, acc_ref, lo, hi):
    acts = []
    for c in range(lo * FF_CHUNK, hi * FF_CHUNK, FF_CHUNK):
        gch = jnp.dot(hb, wfi_ref[:, c:c + FF_CHUNK], preferred_element_type=F32)
        uch = jnp.dot(hb, wfi_ref[:, D_FF + c:D_FF + c + FF_CHUNK], preferred_element_type=F32)
        acts.append((jax.nn.silu(gch) * uch).astype(BF16))
    act = jnp.concatenate(acts, axis=1) if len(acts) > 1 else acts[0]
    acc_ref[...] += jnp.dot(act, wfo_ref[lo * FF_CHUNK:hi * FF_CHUNK, :], preferred_element_type=F32)


def _ffn_finish(x1, ffn_out, p_ref, gp_ref, wpg_ref, wpp_ref, gfin_ref, y_ref):
    x2 = x1 + ffn_out
    hp = _rms(x2, gp_ref[...]).astype(BF16)
    gate = jax.nn.sigmoid(jnp.dot(hp, wpg_ref[...], preferred_element_type=F32))
    pe = jnp.dot(p_ref[...].astype(BF16), wpp_ref[...], preferred_element_type=F32)
    x3 = x2 + gate * pe
    y_ref[...] = _rms(x3, gfin_ref[...])


def _ffn_kernel(x_ref, p_ref, gf_ref, wfi_ref, wfo_ref, gp_ref, wpg_ref, wpp_ref, gfin_ref, y_ref, acc_ref):
    x1 = x_ref[...]
    hb = _rms(x1, gf_ref[...]).astype(BF16)
    acc_ref[...] = jnp.zeros_like(acc_ref)
    _ffn_chunks(hb, wfi_ref, wfo_ref, acc_ref, 0, D_FF // FF_CHUNK)
    _ffn_finish(x1, acc_ref[...], p_ref, gp_ref, wpg_ref, wpp_ref, gfin_ref, y_ref)


def _ffn(x1, p, gf, wfi, wfo, gp, wpg, wpp, gfin, tm):
    rows = x1.shape[0]
    row = lambda n: pl.BlockSpec((tm, n), lambda i: (i, 0))
    return pl.pallas_call(
        _ffn_kernel,
        grid=(rows // tm,),
        in_specs=[row(D_MODEL), row(PLE_DIM), _const_spec((1, D_MODEL)),
                  _const_spec((D_MODEL, 2 * D_FF)), _const_spec((D_FF, D_MODEL)),
                  _const_spec((1, D_MODEL)), _const_spec((D_MODEL, D_MODEL)),
                  _const_spec((PLE_DIM, D_MODEL)), _const_spec((1, D_MODEL))],
        out_specs=row(D_MODEL),
        out_shape=jax.ShapeDtypeStruct((rows, D_MODEL), F32),
        scratch_shapes=[pltpu.VMEM((tm, D_MODEL), F32)],
        compiler_params=_params(1),
        name="ffn_ple",
    )(x1, p, gf, wfi, wfo, gp, wpg, wpp, gfin)


SAMPLE_CONV_BLOCK = 32


def _sample_conv_kernel(state_ref, u_ref, wdw_ref, bdw_ref, c_ref, new_ref):
    ctx = CONV_WIDTH - 1
    for lc in range(N_SLABS):
        cols = slice(lc * LANES, (lc + 1) * LANES)
        u = u_ref[lc]
        acc = u * wdw_ref[ctx:ctx + 1, cols] + bdw_ref[:, cols]
        for k in range(ctx):
            acc = acc + state_ref[k, :, cols] * wdw_ref[k:k + 1, cols]
        c_ref[:, cols] = acc
        new_ref[ctx - 1, :, cols] = u
    for k in range(ctx - 1):
        new_ref[k] = state_ref[k + 1]


def _sample_conv(state_t, u_slabs, wdw, bdw):
    ctx, n, _ = state_t.shape
    sb = SAMPLE_CONV_BLOCK
    return pl.pallas_call(
        _sample_conv_kernel,
        grid=(n // sb,),
        in_specs=[pl.BlockSpec((ctx, sb, C_CONV), lambda i: (0, i, 0)),
                  pl.BlockSpec((N_SLABS, sb, LANES), lambda i: (0, i, 0)),
                  _const_spec((CONV_WIDTH, C_CONV)), _const_spec((1, C_CONV))],
        out_specs=[pl.BlockSpec((sb, C_CONV), lambda i: (i, 0)),
                   pl.BlockSpec((ctx, sb, C_CONV), lambda i: (0, i, 0))],
        out_shape=[jax.ShapeDtypeStruct((n, C_CONV), F32),
                   jax.ShapeDtypeStruct((ctx, n, C_CONV), F32)],
        compiler_params=_params(1),
        name="sample_conv",
    )(state_t, u_slabs, wdw, bdw)


def _cache_sample(q_ref, new_ref, cache_ref, out_ref, o_ref, lse_ref, s, sample, dil):
    length = cache_ref.shape[-1]
    pos = lax.broadcasted_iota(jnp.int32, (1, length), 1)
    used = (pos & (dil - 1)) == 0
    lane = lax.broadcasted_iota(jnp.int32, (HEAD_DIM, LANES), 1)
    last = lane == LANES - 1
    mine = lane == sample

    def pick(ref, r0):
        return jnp.sum(jnp.where(mine, ref[r0:r0 + HEAD_DIM, :], 0.0), axis=1, keepdims=True)

    def shift(kv, h, new):
        rolled = pltpu.roll(cache_ref[s, kv, h], length - 1, 1)
        out_ref[s, kv, h] = rolled
        out_ref[s, kv, h, :, length - LANES:] = jnp.where(last, new, rolled[:, length - LANES:])

    heads = range(H_G)
    q = [pick(q_ref, h * HEAD_DIM) for h in heads]
    k_new = [pick(new_ref, h * HEAD_DIM) for h in heads]
    v_new = [pick(new_ref, GROUP_WIDTH + h * HEAD_DIM) for h in heads]
    shift(0, 0, k_new[0])
    shift(1, 0, v_new[0])
    sc = [jnp.where(used, jnp.sum(cache_ref[s, 0, h] * q[h], axis=0, keepdims=True), MASKED) for h in heads]
    s_new = [jnp.sum(k_new[h] * q[h], axis=0, keepdims=True) for h in heads]
    m = [jnp.maximum(jnp.max(sc[h], axis=1, keepdims=True), s_new[h]) for h in heads]
    shift(0, 1, k_new[1])
    shift(1, 1, v_new[1])
    p = [jnp.exp(sc[h] - m[h]) for h in heads]
    p_new = [jnp.exp(s_new[h] - m[h]) for h in heads]
    l = [jnp.sum(p[h], axis=1, keepdims=True) + p_new[h] for h in heads]
    shift(0, 2, k_new[2])
    shift(1, 2, v_new[2])
    o = [(jnp.sum(cache_ref[s, 1, h] * p[h], axis=1, keepdims=True) + v_new[h] * p_new[h]) / l[h]
         for h in heads]
    shift(0, 3, k_new[3])
    shift(1, 3, v_new[3])
    for h in heads:
        rows = slice(h * HEAD_DIM, (h + 1) * HEAD_DIM)
        o_ref[rows, :] = jnp.where(mine, o[h], o_ref[rows, :])
        lse_ref[rows, :] = jnp.where(mine, m[h] + jnp.log(l[h]), lse_ref[rows, :])


FFN_PARTS = 4
FUSED_VMEM_LIMIT = 57 * 1024 * 1024


def _ffn_cache_kernel(x_ref, p_ref, gf_ref, wfi_ref, wfo_ref, gp_ref, wpg_ref, wpp_ref, gfin_ref, *refs):
    ng = N_GROUPS
    q_refs, new_refs, cache_refs = refs[0:ng], refs[ng:2 * ng], refs[2 * ng:3 * ng]
    y_ref = refs[3 * ng]
    out_refs, o_refs, lse_refs = refs[3 * ng + 1:4 * ng + 1], refs[4 * ng + 1:5 * ng + 1], refs[5 * ng + 1:6 * ng + 1]
    (hb_ref,) = refs[6 * ng + 1:]
    acc_ref = y_ref
    tile = pl.program_id(0)
    part = pl.program_id(1)

    @pl.when((tile == 0) & (part == 0))
    def _():
        for r in o_refs + lse_refs:
            r[...] = jnp.zeros_like(r)

    n_chunks = D_FF // FF_CHUNK
    bounds = [(n_chunks + 2) * j // FFN_PARTS for j in range(FFN_PARTS)] + [n_chunks]

    for j in range(FFN_PARTS):
        @pl.when(part == j)
        def _(j=j):
            if j == 0:
                hb_ref[...] = _rms(x_ref[...], gf_ref[...]).astype(BF16)
                acc_ref[...] = jnp.zeros_like(acc_ref)
            _ffn_chunks(hb_ref[...], wfi_ref, wfo_ref, acc_ref, bounds[j], bounds[j + 1])
            if j == FFN_PARTS - 1:
                _ffn_finish(x_ref[...], acc_ref[...], p_ref, gp_ref, wpg_ref, wpp_ref, gfin_ref, y_ref)
            for g in reversed(range(ng)):
                _cache_sample(q_refs[g], new_refs[g], cache_refs[g], out_refs[g], o_refs[g], lse_refs[g],
                              0, tile * FFN_PARTS + part, GROUPS[g][1])


def _ffn_cache(x1, p, gf, wfi, wfo, gp, wpg, wpp, gfin, q_t, news, caches):
    tm = PROMPT_TM
    rows = x1.shape[0]
    n = caches[0].shape[0]
    assert (rows // tm) * FFN_PARTS == n
    row = lambda w: pl.BlockSpec((tm, w), lambda i, k: (i, 0))
    bigs = [pl.BlockSpec((1,) + c.shape[1:], lambda i, k: (i * FFN_PARTS + k, 0, 0, 0, 0)) for c in caches]
    res = pl.BlockSpec((GROUP_WIDTH, n), lambda i, k: (0, 0))
    res_shape = jax.ShapeDtypeStruct((GROUP_WIDTH, n), F32)
    ng = N_GROUPS
    return pl.pallas_call(
        _ffn_cache_kernel,
        grid=(rows // tm, FFN_PARTS),
        in_specs=[row(D_MODEL), row(PLE_DIM), _const_spec((1, D_MODEL)),
                  _const_spec((D_MODEL, 2 * D_FF)), _const_spec((D_FF, D_MODEL)),
                  _const_spec((1, D_MODEL)), _const_spec((D_MODEL, D_MODEL)),
                  _const_spec((PLE_DIM, D_MODEL)), _const_spec((1, D_MODEL))]
                 + [pl.BlockSpec((GROUP_WIDTH, n), lambda i, k, g=g: (g, 0), pipeline_mode=pl.Buffered(1))
                    for g in range(ng)]
                 + [_const_spec((2 * GROUP_WIDTH, n))] * ng + bigs,
        out_specs=[row(D_MODEL)] + bigs + [res] * (2 * ng),
        out_shape=[jax.ShapeDtypeStruct((rows, D_MODEL), F32)]
                  + [jax.ShapeDtypeStruct(c.shape, F32) for c in caches] + [res_shape] * (2 * ng),
        scratch_shapes=[pltpu.VMEM((tm, D_MODEL), BF16)],
        compiler_params=pltpu.CompilerParams(dimension_semantics=("arbitrary", "arbitrary"),
                                             vmem_limit_bytes=FUSED_VMEM_LIMIT),
        name="ffn_ple_caches",
    )(x1, p, gf, wfi, wfo, gp, wpg, wpp, gfin, *([q_t] * ng), *news, *caches)


def _rope_tables(pos):
    half = ROT_DIM // 2
    inv_freq = ROPE_THETA ** (-jnp.arange(half, dtype=F32) / half)
    ang = pos.astype(F32)[:, None] * inv_freq[None, :]
    cos, sin = jnp.cos(ang), jnp.sin(ang)
    n = pos.shape[0]
    zeros = lambda w: jnp.zeros((n, w), F32)
    cos_h = jnp.concatenate([cos, cos, jnp.ones((n, HEAD_DIM - ROT_DIM), F32)], axis=-1)
    sa_h = jnp.concatenate([-sin, zeros(HEAD_DIM - half)], axis=-1)
    sb_h = jnp.concatenate([zeros(half), sin, zeros(HEAD_DIM - ROT_DIM)], axis=-1)
    return tuple(jnp.tile(t, (1, H_G)) for t in (cos_h, sa_h, sb_h))


def _position_major(t):
    b, _, length = t.shape
    return jnp.transpose(t.reshape(b, 2, H_G, HEAD_DIM, length), (0, 4, 1, 2, 3))[None]


def kernel(x_prompt, x_sample, state_conv, cache_win_a, cache_win_b, cache_win_c, p_prompt, p_sample,
           w_in, g_mix, w_dw, b_dw, ln_g, ln_b, w_conv_out, w_attn_out, w_o, g_ffn, w_ffn_in, w_ffn_out,
           g_ple, w_ple_gate, w_ple_proj, g_final):
    assert w_in.shape[0] == 1, "single-layer step"
    w_in_b = w_in[0].astype(BF16)
    wco = w_conv_out[0].astype(BF16)
    wao = w_attn_out[0].astype(BF16)
    wo = w_o[0].astype(BF16)
    wfi = w_ffn_in[0].astype(BF16)
    wfo = w_ffn_out[0].astype(BF16)
    wpg = w_ple_gate[0].astype(BF16)
    wpp = w_ple_proj[0].astype(BF16)
    gfin = g_final.reshape(1, D_MODEL)
    n_prompt = BATCH * SEQ
    ctx = CONV_WIDTH - 1

    xp = x_prompt.reshape(n_prompt, D_MODEL)
    tabs_p = _rope_tables(jnp.arange(SEQ, dtype=jnp.int32))
    c_slabs, q, kva, kvb, kvc, gates, tail_a, tail_b, tail_c, u_tail = _in_proj_prompt(
        xp, g_mix, w_in_b, *tabs_p, w_dw[0], b_dw)
    per_batch = lambda t: t.reshape(BATCH, SEQ, t.shape[-1])
    o = _attention(per_batch(q), per_batch(kva), per_batch(kvb), per_batch(kvc))
    x1 = _prompt_mix(c_slabs, xp, o.reshape(n_prompt, GROUP_WIDTH), gates, ln_g, ln_b, wco, wao, wo)
    new_conv_prompt = u_tail[None, :, CONV_HALO - ctx:, :]

    n_s = DEC_BATCH
    xs = x_sample.reshape(n_s, D_MODEL)
    tabs_s = tuple(jnp.broadcast_to(t, (n_s, GROUP_WIDTH))
                   for t in _rope_tables(jnp.full((1,), PAST_LEN, jnp.int32)))
    u_s, q_t, new_a, new_b, new_c, gates_s = _in_proj_sample(xs, g_mix, w_in_b, *tabs_s)
    c_s, new_conv_t = _sample_conv(jnp.transpose(state_conv[0], (1, 0, 2)), u_s, w_dw[0], b_dw)
    caches_t = [jnp.transpose(c[0], (0, 2, 3, 4, 1)) for c in (cache_win_a, cache_win_b, cache_win_c)]
    fused = _ffn_cache(x1, p_prompt.reshape(n_prompt, PLE_DIM), g_ffn, wfi, wfo, g_ple, wpg, wpp, gfin,
                       q_t, [new_a, new_b, new_c], caches_t)
    y_prompt = fused[0]
    new_wins = [jnp.transpose(t, (0, 4, 1, 2, 3))[None] for t in fused[1:1 + N_GROUPS]]
    ogs = fused[1 + N_GROUPS:1 + 2 * N_GROUPS]
    lgs = fused[1 + 2 * N_GROUPS:]
    x1_s = _sample_mix(c_s, xs, ogs, lgs, gates_s, ln_g, ln_b, wco, wao, wo)
    y_sample = _ffn(x1_s, p_sample.reshape(n_s, PLE_DIM), g_ffn, wfi, wfo, g_ple, wpg, wpp, gfin,
                    n_s).reshape(n_s, 1, D_MODEL)
    new_conv_sample = jnp.transpose(new_conv_t, (1, 0, 2))[None]

    return (y_prompt.reshape(BATCH, SEQ, D_MODEL), y_sample, new_conv_prompt, _position_major(tail_a), _position_major(tail_b),
            _position_major(tail_c), new_conv_sample, new_wins[0], new_wins[1], new_wins[2])
```

```python
import jax
import jax.numpy as jnp
from jax import lax
from jax.experimental import pallas as pl
from jax.experimental.pallas import tpu as pltpu

D_MODEL = 1024
BATCH = 8
SEQ = 2048
DEC_BATCH = 128
PAST_LEN = 8192
HEAD_DIM = 64
GROUPS = ((128, 1), (512, 4), (2048, 16))
H_G = 4
N_GROUPS = len(GROUPS)
GROUP_WIDTH = H_G * HEAD_DIM
ATTN_WIDTH = N_GROUPS * GROUP_WIDTH
ROT_DIM = HEAD_DIM // 4
ROPE_THETA = 500000.0
C_CONV = D_MODEL
CONV_WIDTH = 31
D_FF = 2816
PLE_DIM = 256
NORM_EPS = 1e-6
NK = 128
Q_OFF = 2 * C_CONV
K_OFF = Q_OFF + ATTN_WIDTH
V_OFF = K_OFF + ATTN_WIDTH
GATE_OFF = V_OFF + ATTN_WIDTH
IN_COLS = GATE_OFF + 2 * D_MODEL

F32 = jnp.float32
BF16 = jnp.bfloat16
LANES = 128
SUBLANES = 8
N_SLABS = C_CONV // LANES
MASKED = -jnp.inf
VMEM_LIMIT = 52 * 1024 * 1024


def _const_spec(shape):
    return pl.BlockSpec(shape, lambda *_: (0,) * len(shape), pipeline_mode=pl.Buffered(1))


def _params(n_axes):
    return pltpu.CompilerParams(dimension_semantics=("arbitrary",) * n_axes,
                                vmem_limit_bytes=VMEM_LIMIT)


def _rms(x, g):
    return x * lax.rsqrt(jnp.mean(x * x, axis=-1, keepdims=True) + NORM_EPS) * g


def _sigmoid(x):
    return 0.5 * jnp.tanh(0.5 * x) + 0.5


PROMPT_TM = 512
TILES_PER_SEQ = SEQ // PROMPT_TM
CONV_HALO = 32
CONV_PITCH = 4
CONV_GROUP = CONV_PITCH * SUBLANES
CONV_UNROLL = 1


def _projector(x_ref, g_ref, w_ref, cos_ref, sa_ref, sb_ref):
    hb = _rms(x_ref[...], g_ref[...]).astype(BF16)
    ch = GROUP_WIDTH

    def mm(c0):
        return jnp.dot(hb, w_ref[:, c0:c0 + ch], preferred_element_type=F32)

    def glu(c):
        return mm(c) * _sigmoid(mm(C_CONV + c))

    def gates(gate_ref, lo=0, hi=2 * D_MODEL // ch):
        for c in range(lo * ch, hi * ch, ch):
            gate_ref[:, c:c + ch] = _sigmoid(mm(GATE_OFF + c)).astype(BF16)

    def rope(z):
        return (z * cos_ref[...] + pltpu.roll(z, ch - ROT_DIM // 2, 1) * sa_ref[...]
                + pltpu.roll(z, ROT_DIM // 2, 1) * sb_ref[...])

    def qkv(g):
        c = g * ch
        return rope(mm(Q_OFF + c)) * (HEAD_DIM ** -0.5), rope(mm(K_OFF + c)), mm(V_OFF + c)

    return qkv, glu, gates


def _conv_slab(win_ref, lc, wdw_ref, bdw_ref, c_ref, n_rows):
    first = CONV_HALO - (CONV_WIDTH - 1)
    cols = slice(lc * LANES, (lc + 1) * LANES)
    n_acc = CONV_UNROLL * CONV_PITCH
    offs = [(i // CONV_PITCH) * CONV_GROUP + i % CONV_PITCH for i in range(n_acc)]
    bias = jnp.broadcast_to(bdw_ref[:, cols], (SUBLANES, LANES))
    for base in range(0, n_rows, CONV_UNROLL * CONV_GROUP):
        accs = [bias] * n_acc
        for k in range(CONV_WIDTH):
            wk = wdw_ref[k:k + 1, cols]
            for i in range(n_acc):
                tap = win_ref[lc, pl.ds(base + first + k + offs[i], SUBLANES, stride=CONV_PITCH), :]
                accs[i] = accs[i] + tap * wk
        for i in range(n_acc):
            c_ref[lc, pl.ds(base + offs[i], SUBLANES, stride=CONV_PITCH), :] = accs[i]


def _in_proj_prompt_kernel(x_ref, g_ref, w_ref, cos_ref, sa_ref, sb_ref, wdw_ref, bdw_ref,
                           c_ref, q_ref, kva_ref, kvb_ref, kvc_ref, gate_ref, ta_ref, tb_ref, tc_ref, ut_ref,
                           win_ref):
    tm = PROMPT_TM
    ch = GROUP_WIDTH

    @pl.when(pl.program_id(0) == 0)
    def _():
        win_ref[:, tm:tm + CONV_HALO, :] = jnp.zeros((N_SLABS, CONV_HALO, LANES), F32)

    qkv, glu, gates = _projector(x_ref, g_ref, w_ref, cos_ref, sa_ref, sb_ref)
    first_tile = pl.program_id(0) % TILES_PER_SEQ == 0
    def head_group(g):
        kv_ref, t_ref = ((kva_ref, ta_ref), (kvb_ref, tb_ref), (kvc_ref, tc_ref))[g]
        q, k, v = qkv(g)
        q_ref[:, g * ch:(g + 1) * ch] = q
        kv_ref[:, 0:ch] = k
        kv_ref[:, ch:2 * ch] = v
        n = t_ref.shape[1]
        t_ref[0:ch, :] = k[tm - n:, :].T
        t_ref[ch:2 * ch, :] = v[tm - n:, :].T

    def glu_chunk(i):
        c = i * ch
        u = glu(c)
        ut_ref[:, c:c + ch] = u[tm - CONV_HALO:, :]
        for j in range(ch // LANES):
            lc = c // LANES + j
            win_ref[lc, 0:CONV_HALO, :] = jnp.where(first_tile, 0.0, win_ref[lc, tm:tm + CONV_HALO, :])
            win_ref[lc, CONV_HALO:CONV_HALO + tm, :] = u[:, j * LANES:(j + 1) * LANES]

    n_gate = 2 * D_MODEL // ch
    after_slab = [
        lambda: glu_chunk(1), lambda: head_group(0), lambda: glu_chunk(2), lambda: head_group(1),
        lambda: glu_chunk(3), lambda: head_group(2),
        lambda: gates(gate_ref, 0, n_gate // 2), lambda: gates(gate_ref, n_gate // 2, n_gate)]
    glu_chunk(0)
    for lc in range(N_SLABS):
        _conv_slab(win_ref, lc, wdw_ref, bdw_ref, c_ref, tm)
        after_slab[lc]()


def _in_proj_prompt(x2d, g, w, cos, sa, sb, wdw, bdw):
    tm = PROMPT_TM
    rows = x2d.shape[0]
    assert GROUPS[1][0] == tm and GROUPS[2][0] == SEQ and GROUPS[0][0] <= tm
    row = lambda n: pl.BlockSpec((tm, n), lambda i: (i, 0))
    tab = pl.BlockSpec((tm, GROUP_WIDTH), lambda i: (i % TILES_PER_SEQ, 0))
    kv_w = 2 * GROUP_WIDTH
    tail = lambda n: pl.BlockSpec((None, kv_w, n), lambda i: (i // TILES_PER_SEQ, 0, 0))
    return pl.pallas_call(
        _in_proj_prompt_kernel,
        grid=(rows // tm,),
        in_specs=[row(D_MODEL), _const_spec((1, D_MODEL)), _const_spec((D_MODEL, IN_COLS)), tab, tab, tab,
                  _const_spec((CONV_WIDTH, C_CONV)), _const_spec((1, C_CONV))],
        out_specs=[pl.BlockSpec((N_SLABS, tm, LANES), lambda i: (0, i, 0)),
                   row(ATTN_WIDTH), row(kv_w), row(kv_w), row(kv_w), row(2 * D_MODEL),
                   tail(GROUPS[0][0]), tail(GROUPS[1][0]),
                   pl.BlockSpec((None, kv_w, tm), lambda i: (i // TILES_PER_SEQ, 0, i % TILES_PER_SEQ)),
                   pl.BlockSpec((None, CONV_HALO, C_CONV), lambda i: (i // TILES_PER_SEQ, 0, 0))],
        out_shape=[jax.ShapeDtypeStruct((N_SLABS, rows, LANES), F32),
                   jax.ShapeDtypeStruct((rows, ATTN_WIDTH), F32),
                   jax.ShapeDtypeStruct((rows, kv_w), F32),
                   jax.ShapeDtypeStruct((rows, kv_w), F32),
                   jax.ShapeDtypeStruct((rows, kv_w), F32),
                   jax.ShapeDtypeStruct((rows, 2 * D_MODEL), BF16),
                   jax.ShapeDtypeStruct((BATCH, kv_w, GROUPS[0][0]), F32),
                   jax.ShapeDtypeStruct((BATCH, kv_w, GROUPS[1][0]), F32),
                   jax.ShapeDtypeStruct((BATCH, kv_w, SEQ), F32),
                   jax.ShapeDtypeStruct((BATCH, CONV_HALO, C_CONV), F32)],
        scratch_shapes=[pltpu.VMEM((N_SLABS, CONV_HALO + tm, LANES), F32)],
        compiler_params=_params(1),
        name="in_proj_prompt",
    )(x2d, g, w, cos, sa, sb, wdw, bdw)


def _in_proj_sample_kernel(x_ref, g_ref, w_ref, cos_ref, sa_ref, sb_ref,
                           u_ref, qt_ref, ta_ref, tb_ref, tc_ref, gate_ref):
    qkv, glu, gates = _projector(x_ref, g_ref, w_ref, cos_ref, sa_ref, sb_ref)
    ch = GROUP_WIDTH
    for g, t_ref in enumerate((ta_ref, tb_ref, tc_ref)):
        q, k, v = qkv(g)
        qt_ref[g * ch:(g + 1) * ch, :] = q.T
        t_ref[0:ch, :] = k.T
        t_ref[ch:2 * ch, :] = v.T
    for c in range(0, C_CONV, ch):
        u = glu(c)
        for j in range(ch // LANES):
            u_ref[c // LANES + j] = u[:, j * LANES:(j + 1) * LANES]
    gates(gate_ref)


def _in_proj_sample(x2d, g, w, cos, sa, sb):
    n = x2d.shape[0]
    full = lambda r, c: pl.BlockSpec((r, c), lambda i: (0, 0))
    kv_w = 2 * GROUP_WIDTH
    return pl.pallas_call(
        _in_proj_sample_kernel,
        grid=(1,),
        in_specs=[full(n, D_MODEL), _const_spec((1, D_MODEL)), _const_spec((D_MODEL, IN_COLS)),
                  full(n, GROUP_WIDTH), full(n, GROUP_WIDTH), full(n, GROUP_WIDTH)],
        out_specs=[pl.BlockSpec((N_SLABS, n, LANES), lambda i: (0, 0, 0)),
                   full(ATTN_WIDTH, n), full(kv_w, n), full(kv_w, n), full(kv_w, n), full(n, 2 * D_MODEL)],
        out_shape=[jax.ShapeDtypeStruct((N_SLABS, n, LANES), F32),
                   jax.ShapeDtypeStruct((ATTN_WIDTH, n), F32),
                   jax.ShapeDtypeStruct((kv_w, n), F32),
                   jax.ShapeDtypeStruct((kv_w, n), F32),
                   jax.ShapeDtypeStruct((kv_w, n), F32),
                   jax.ShapeDtypeStruct((n, 2 * D_MODEL), BF16)],
        compiler_params=_params(1),
        name="in_proj_sample",
    )(x2d, g, w, cos, sa, sb)


ATTN_UNROLL = 8


def _combine_groups(lses, outs):
    mx = jnp.maximum(jnp.maximum(lses[0], lses[1]), lses[2])
    es = [jnp.exp(l - mx) for l in lses]
    num = es[0] * outs[0] + es[1] * outs[1] + es[2] * outs[2]
    return num / (es[0] + es[1] + es[2])


def _largest_divisor(n, cap):
    return max(d for d in range(1, cap + 1) if n % d == 0)


def _attn_kernel(qa_ref, qb_ref, qc_ref, ka_ref, va_ref, kb_ref, vb_ref, kc_ref, vc_ref,
                 o_ref, og_ref, lg_ref):
    lane = lax.broadcasted_iota(jnp.int32, (NK, LANES), 1)
    qi = lax.broadcasted_iota(jnp.int32, (NK, LANES), 0)
    first_head = lane < HEAD_DIM
    own_mask = lane <= qi
    prev_mask = lane >= qi
    nt = (((1,), (1,)), ((), ()))

    def blocks(g, q_ref, k_ref, v_ref, row_list):
        scores, values = [], []
        for rows, prev_rows in row_list:
            qb = q_ref[rows, :]
            q2 = jnp.concatenate([jnp.where(first_head, qb, 0.0), jnp.where(first_head, 0.0, qb)],
                                 axis=0).astype(BF16)
            keys = k_ref[rows, :].astype(BF16)
            vals = v_ref[rows, :].astype(BF16)
            mask = own_mask
            if prev_rows is not None:
                keys = jnp.concatenate([keys, k_ref[prev_rows, :].astype(BF16)], axis=0)
                vals = jnp.concatenate([vals, v_ref[prev_rows, :].astype(BF16)], axis=0)
                mask = jnp.concatenate([own_mask, prev_mask], axis=1)
            values.append(jnp.concatenate([vals, jnp.ones_like(vals)], axis=1))
            s = lax.dot_general(q2, keys, nt, preferred_element_type=F32)
            scores.append(jnp.where(jnp.concatenate([mask, mask], axis=0), s, MASKED))
        probs, maxes = [], []
        for s in scores:
            m = jnp.max(s, axis=-1, keepdims=True)
            probs.append(jnp.exp(s - m).astype(BF16))
            maxes.append(m)
        for b, (rows, _) in enumerate(row_list):
            r = jnp.dot(probs[b], values[b], preferred_element_type=F32)
            l = r[:, LANES:]
            out = r[:, :LANES] / l
            lse = maxes[b] + jnp.log(l)
            og_ref[g, rows, :] = jnp.where(first_head, out[:NK], out[NK:])
            lg_ref[g, rows, :] = jnp.where(first_head, lse[:NK], lse[NK:])

    for g, (q_ref, k_ref, v_ref) in enumerate(((qa_ref, ka_ref, va_ref), (qb_ref, kb_ref, vb_ref),
                                               (qc_ref, kc_ref, vc_ref))):
        dil = GROUPS[g][1]
        span = NK * dil
        n_later = dil * (SEQ // span - 1)

        def rows_at(start, dil=dil):
            return pl.ds(start, NK) if dil == 1 else pl.ds(start, NK, stride=dil)

        n_first = _largest_divisor(dil, ATTN_UNROLL)

        def first(t, carry, g=g, q_ref=q_ref, k_ref=k_ref, v_ref=v_ref, rows_at=rows_at, n=n_first):
            blocks(g, q_ref, k_ref, v_ref, [(rows_at(t * n + j), None) for j in range(n)])
            return carry

        lax.fori_loop(0, dil // n_first, first, 0)
        if n_later:
            n_per = _largest_divisor(n_later, ATTN_UNROLL)

            def later(t, carry, g=g, q_ref=q_ref, k_ref=k_ref, v_ref=v_ref, rows_at=rows_at, dil=dil,
                      span=span, n=n_per):
                row_list = []
                for j in range(n):
                    i = t * n + j
                    start = i % dil + (1 + i // dil) * span
                    row_list.append((rows_at(start), rows_at(start - span)))
                blocks(g, q_ref, k_ref, v_ref, row_list)
                return carry

            lax.fori_loop(0, n_later // n_per, later, 0)

    tc = 256
    for t in range(0, SEQ, tc):
        rows = slice(t, t + tc)
        o_ref[rows, :] = _combine_groups([lg_ref[g, rows, :] for g in range(N_GROUPS)],
                                         [og_ref[g, rows, :] for g in range(N_GROUPS)]).astype(o_ref.dtype)


def _attention(q3, kva3, kvb3, kvc3):
    def col(c):
        return pl.BlockSpec((None, SEQ, LANES), lambda b, hp, c=c: (b, 0, c + hp))
    n_pairs = GROUP_WIDTH // LANES
    return pl.pallas_call(
        _attn_kernel,
        grid=(BATCH, n_pairs),
        in_specs=[col(0), col(n_pairs), col(2 * n_pairs),
                  col(0), col(n_pairs), col(0), col(n_pairs), col(0), col(n_pairs)],
        out_specs=pl.BlockSpec((None, SEQ, LANES), lambda b, hp: (b, 0, hp)),
        out_shape=jax.ShapeDtypeStruct((BATCH, SEQ, GROUP_WIDTH), BF16),
        scratch_shapes=[pltpu.VMEM((N_GROUPS, SEQ, LANES), F32), pltpu.VMEM((N_GROUPS, SEQ, LANES), F32)],
        compiler_params=_params(2),
        name="prompt_attention",
    )(q3, q3, q3, kva3, kva3, kvb3, kvb3, kvc3, kvc3)


def _merge(c, x, ob, gate_ref, lng_ref, lnb_ref, wco_ref, wao_ref, wo_ref):
    mu = jnp.mean(c, axis=-1, keepdims=True)
    d = c - mu
    var = jnp.mean(d * d, axis=-1, keepdims=True)
    y = d * lax.rsqrt(var + NORM_EPS) * lng_ref[...] + lnb_ref[...]
    a_out = jnp.dot(jax.nn.silu(y).astype(BF16), wco_ref[...], preferred_element_type=F32)
    b_out = jnp.dot(ob, wao_ref[...], preferred_element_type=F32)
    merged = (gate_ref[:, 0:D_MODEL].astype(F32) * a_out
              + gate_ref[:, D_MODEL:2 * D_MODEL].astype(F32) * b_out)
    return x + jnp.dot(merged.astype(BF16), wo_ref[...], preferred_element_type=F32)


MIX_TM = 1024


def _prompt_mix_kernel(c_ref, x_ref, o_ref, gate_ref, lng_ref, lnb_ref, wco_ref, wao_ref, wo_ref, x1_ref):
    c = jnp.concatenate([c_ref[lc] for lc in range(N_SLABS)], axis=-1)
    x1_ref[...] = _merge(c, x_ref[...], o_ref[...], gate_ref, lng_ref, lnb_ref, wco_ref, wao_ref, wo_ref)


def _prompt_mix(c_slabs, x2d, o2d, gate2d, lng, lnb, wco, wao, wo):
    tm = MIX_TM
    rows = x2d.shape[0]
    row = lambda n: pl.BlockSpec((tm, n), lambda i: (i, 0))
    return pl.pallas_call(
        _prompt_mix_kernel,
        grid=(rows // tm,),
        in_specs=[pl.BlockSpec((N_SLABS, tm, LANES), lambda i: (0, i, 0)),
                  row(D_MODEL), row(GROUP_WIDTH), row(2 * D_MODEL),
                  _const_spec((1, C_CONV)), _const_spec((1, C_CONV)),
                  _const_spec((C_CONV, D_MODEL)), _const_spec((GROUP_WIDTH, D_MODEL)),
                  _const_spec((D_MODEL, D_MODEL))],
        out_specs=row(D_MODEL),
        out_shape=jax.ShapeDtypeStruct((rows, D_MODEL), F32),
        compiler_params=_params(1),
        name="prompt_mix",
    )(c_slabs, x2d, o2d, gate2d, lng, lnb, wco, wao, wo)


def _sample_mix_kernel(c_ref, x_ref, oa_ref, ob_ref, oc_ref, la_ref, lb_ref, lc_ref, gate_ref,
                       lng_ref, lnb_ref, wco_ref, wao_ref, wo_ref, x1_ref):
    ob = _combine_groups([r[...].T for r in (la_ref, lb_ref, lc_ref)],
                         [r[...].T for r in (oa_ref, ob_ref, oc_ref)]).astype(BF16)
    x1_ref[...] = _merge(c_ref[...], x_ref[...], ob, gate_ref, lng_ref, lnb_ref, wco_ref, wao_ref, wo_ref)


def _sample_mix(c, x, ogs, lgs, gate, lng, lnb, wco, wao, wo):
    rows = x.shape[0]
    full = lambda n: _const_spec((rows, n))
    per_group = _const_spec((GROUP_WIDTH, rows))
    return pl.pallas_call(
        _sample_mix_kernel,
        grid=(1,),
        in_specs=[full(C_CONV), full(D_MODEL)] + [per_group] * (2 * N_GROUPS) + [full(2 * D_MODEL),
                  _const_spec((1, C_CONV)), _const_spec((1, C_CONV)),
                  _const_spec((C_CONV, D_MODEL)), _const_spec((GROUP_WIDTH, D_MODEL)),
                  _const_spec((D_MODEL, D_MODEL))],
        out_specs=pl.BlockSpec((rows, D_MODEL), lambda i: (0, 0)),
        out_shape=jax.ShapeDtypeStruct((rows, D_MODEL), F32),
        compiler_params=_params(1),
        name="sample_mix",
    )(c, x, *ogs, *lgs, gate, lng, lnb, wco, wao, wo)


FF_CHUNK = 256
FF_ROWS = 256


def _ffn_chunks(hb, wfi_ref, wfo_ref, acc_ref, lo, hi):
    for r in range(0, hb.shape[0], FF_ROWS):
        rows = slice(r, min(r + FF_ROWS, hb.shape[0]))
        hr = hb[rows, :]
        acts = []
        for c in range(lo * FF_CHUNK, hi * FF_CHUNK, FF_CHUNK):
            gch = jnp.dot(hr, wfi_ref[:, c:c + FF_CHUNK], preferred_element_type=F32)
            uch = jnp.dot(hr, wfi_ref[:, D_FF + c:D_FF + c + FF_CHUNK], preferred_element_type=F32)
            acts.append((jax.nn.silu(gch) * uch).astype(BF16))
        act = jnp.concatenate(acts, axis=1) if len(acts) > 1 else acts[0]
        acc_ref[rows, :] += jnp.dot(act, wfo_ref[lo * FF_CHUNK:hi * FF_CHUNK, :], preferred_element_type=F32)


def _ffn_finish(x1, ffn_out, p_ref, gp_ref, wpg_ref, wpp_ref, gfin_ref, y_ref):
    x2 = x1 + ffn_out
    hp = _rms(x2, gp_ref[...]).astype(BF16)
    gate = jax.nn.sigmoid(jnp.dot(hp, wpg_ref[...], preferred_element_type=F32))
    pe = jnp.dot(p_ref[...].astype(BF16), wpp_ref[...], preferred_element_type=F32)
    x3 = x2 + gate * pe
    y_ref[...] = _rms(x3, gfin_ref[...])


def _ffn_kernel(x_ref, p_ref, gf_ref, wfi_ref, wfo_ref, gp_ref, wpg_ref, wpp_ref, gfin_ref, y_ref, acc_ref):
    x1 = x_ref[...]
    hb = _rms(x1, gf_ref[...]).astype(BF16)
    acc_ref[...] = jnp.zeros_like(acc_ref)
    _ffn_chunks(hb, wfi_ref, wfo_ref, acc_ref, 0, D_FF // FF_CHUNK)
    _ffn_finish(x1, acc_ref[...], p_ref, gp_ref, wpg_ref, wpp_ref, gfin_ref, y_ref)


def _ffn(x1, p, gf, wfi, wfo, gp, wpg, wpp, gfin, tm):
    rows = x1.shape[0]
    row = lambda n: pl.BlockSpec((tm, n), lambda i: (i, 0))
    return pl.pallas_call(
        _ffn_kernel,
        grid=(rows // tm,),
        in_specs=[row(D_MODEL), row(PLE_DIM), _const_spec((1, D_MODEL)),
                  _const_spec((D_MODEL, 2 * D_FF)), _const_spec((D_FF, D_MODEL)),
                  _const_spec((1, D_MODEL)), _const_spec((D_MODEL, D_MODEL)),
                  _const_spec((PLE_DIM, D_MODEL)), _const_spec((1, D_MODEL))],
        out_specs=row(D_MODEL),
        out_shape=jax.ShapeDtypeStruct((rows, D_MODEL), F32),
        scratch_shapes=[pltpu.VMEM((tm, D_MODEL), F32)],
        compiler_params=_params(1),
        name="ffn_ple",
    )(x1, p, gf, wfi, wfo, gp, wpg, wpp, gfin)


SAMPLE_CONV_BLOCK = 32


def _sample_conv_kernel(state_ref, u_ref, wdw_ref, bdw_ref, c_ref, new_ref):
    ctx = CONV_WIDTH - 1
    for lc in range(N_SLABS):
        cols = slice(lc * LANES, (lc + 1) * LANES)
        u = u_ref[lc]
        acc = u * wdw_ref[ctx:ctx + 1, cols] + bdw_ref[:, cols]
        for k in range(ctx):
            acc = acc + state_ref[k, :, cols] * wdw_ref[k:k + 1, cols]
        c_ref[:, cols] = acc
        new_ref[ctx - 1, :, cols] = u
    for k in range(ctx - 1):
        new_ref[k] = state_ref[k + 1]


def _sample_conv(state_t, u_slabs, wdw, bdw):
    ctx, n, _ = state_t.shape
    sb = SAMPLE_CONV_BLOCK
    return pl.pallas_call(
        _sample_conv_kernel,
        grid=(n // sb,),
        in_specs=[pl.BlockSpec((ctx, sb, C_CONV), lambda i: (0, i, 0)),
                  pl.BlockSpec((N_SLABS, sb, LANES), lambda i: (0, i, 0)),
                  _const_spec((CONV_WIDTH, C_CONV)), _const_spec((1, C_CONV))],
        out_specs=[pl.BlockSpec((sb, C_CONV), lambda i: (i, 0)),
                   pl.BlockSpec((ctx, sb, C_CONV), lambda i: (0, i, 0))],
        out_shape=[jax.ShapeDtypeStruct((n, C_CONV), F32),
                   jax.ShapeDtypeStruct((ctx, n, C_CONV), F32)],
        compiler_params=_params(1),
        name="sample_conv",
    )(state_t, u_slabs, wdw, bdw)


def _cache_sample(q_ref, new_ref, cache_ref, out_ref, o_ref, lse_ref, s, sample, dil):
    length = cache_ref.shape[-1]
    pos = lax.broadcasted_iota(jnp.int32, (1, length), 1)
    used = (pos & (dil - 1)) == 0
    lane = lax.broadcasted_iota(jnp.int32, (HEAD_DIM, LANES), 1)
    last = lane == LANES - 1
    mine = lane == sample

    def pick(ref, r0):
        return jnp.sum(jnp.where(mine, ref[r0:r0 + HEAD_DIM, :], 0.0), axis=1, keepdims=True)

    def shift(kv, h, new):
        rolled = pltpu.roll(cache_ref[s, kv, h], length - 1, 1)
        out_ref[s, kv, h] = rolled
        out_ref[s, kv, h, :, length - LANES:] = jnp.where(last, new, rolled[:, length - LANES:])

    heads = range(H_G)
    q = [pick(q_ref, h * HEAD_DIM) for h in heads]
    k_new = [pick(new_ref, h * HEAD_DIM) for h in heads]
    v_new = [pick(new_ref, GROUP_WIDTH + h * HEAD_DIM) for h in heads]
    shift(0, 0, k_new[0])
    shift(1, 0, v_new[0])
    sc = [jnp.where(used, jnp.sum(cache_ref[s, 0, h] * q[h], axis=0, keepdims=True), MASKED) for h in heads]
    s_new = [jnp.sum(k_new[h] * q[h], axis=0, keepdims=True) for h in heads]
    m = [jnp.maximum(jnp.max(sc[h], axis=1, keepdims=True), s_new[h]) for h in heads]
    shift(0, 1, k_new[1])
    shift(1, 1, v_new[1])
    p = [jnp.exp(sc[h] - m[h]) for h in heads]
    p_new = [jnp.exp(s_new[h] - m[h]) for h in heads]
    l = [jnp.sum(p[h], axis=1, keepdims=True) + p_new[h] for h in heads]
    shift(0, 2, k_new[2])
    shift(1, 2, v_new[2])
    o = [(jnp.sum(cache_ref[s, 1, h] * p[h], axis=1, keepdims=True) + v_new[h] * p_new[h]) / l[h]
         for h in heads]
    shift(0, 3, k_new[3])
    shift(1, 3, v_new[3])
    for h in heads:
        rows = slice(h * HEAD_DIM, (h + 1) * HEAD_DIM)
        o_ref[rows, :] = jnp.where(mine, o[h], o_ref[rows, :])
        lse_ref[rows, :] = jnp.where(mine, m[h] + jnp.log(l[h]), lse_ref[rows, :])


FFN_PARTS = 4
FUSED_VMEM_LIMIT = 57 * 1024 * 1024


def _ffn_cache_kernel(x_ref, p_ref, gf_ref, wfi_ref, wfo_ref, gp_ref, wpg_ref, wpp_ref, gfin_ref, *refs):
    ng = N_GROUPS
    q_refs, new_refs, cache_refs = refs[0:ng], refs[ng:2 * ng], refs[2 * ng:3 * ng]
    y_ref = refs[3 * ng]
    out_refs, o_refs, lse_refs = refs[3 * ng + 1:4 * ng + 1], refs[4 * ng + 1:5 * ng + 1], refs[5 * ng + 1:6 * ng + 1]
    (hb_ref,) = refs[6 * ng + 1:]
    acc_ref = y_ref
    tile = pl.program_id(0)
    part = pl.program_id(1)

    @pl.when((tile == 0) & (part == 0))
    def _():
        for r in o_refs + lse_refs:
            r[...] = jnp.zeros_like(r)

    n_chunks = D_FF // FF_CHUNK
    bounds = [(n_chunks + 2) * j // FFN_PARTS for j in range(FFN_PARTS)] + [n_chunks]

    for j in range(FFN_PARTS):
        @pl.when(part == j)
        def _(j=j):
            if j == 0:
                hb_ref[...] = _rms(x_ref[...], gf_ref[...]).astype(BF16)
                acc_ref[...] = jnp.zeros_like(acc_ref)
            _ffn_chunks(hb_ref[...], wfi_ref, wfo_ref, acc_ref, bounds[j], bounds[j + 1])
            if j == FFN_PARTS - 1:
                _ffn_finish(x_ref[...], acc_ref[...], p_ref, gp_ref, wpg_ref, wpp_ref, gfin_ref, y_ref)
            for g in reversed(range(ng)):
                _cache_sample(q_refs[g], new_refs[g], cache_refs[g], out_refs[g], o_refs[g], lse_refs[g],
                              0, tile * FFN_PARTS + part, GROUPS[g][1])


def _ffn_cache(x1, p, gf, wfi, wfo, gp, wpg, wpp, gfin, q_t, news, caches):
    tm = PROMPT_TM
    rows = x1.shape[0]
    n = caches[0].shape[0]
    assert (rows // tm) * FFN_PARTS == n
    row = lambda w: pl.BlockSpec((tm, w), lambda i, k: (i, 0))
    bigs = [pl.BlockSpec((1,) + c.shape[1:], lambda i, k: (i * FFN_PARTS + k, 0, 0, 0, 0)) for c in caches]
    res = pl.BlockSpec((GROUP_WIDTH, n), lambda i, k: (0, 0))
    res_shape = jax.ShapeDtypeStruct((GROUP_WIDTH, n), F32)
    ng = N_GROUPS
    return pl.pallas_call(
        _ffn_cache_kernel,
        grid=(rows // tm, FFN_PARTS),
        in_specs=[row(D_MODEL), row(PLE_DIM), _const_spec((1, D_MODEL)),
                  _const_spec((D_MODEL, 2 * D_FF)), _const_spec((D_FF, D_MODEL)),
                  _const_spec((1, D_MODEL)), _const_spec((D_MODEL, D_MODEL)),
                  _const_spec((PLE_DIM, D_MODEL)), _const_spec((1, D_MODEL))]
                 + [pl.BlockSpec((GROUP_WIDTH, n), lambda i, k, g=g: (g, 0), pipeline_mode=pl.Buffered(1))
                    for g in range(ng)]
                 + [_const_spec((2 * GROUP_WIDTH, n))] * ng + bigs,
        out_specs=[row(D_MODEL)] + bigs + [res] * (2 * ng),
        out_shape=[jax.ShapeDtypeStruct((rows, D_MODEL), F32)]
                  + [jax.ShapeDtypeStruct(c.shape, F32) for c in caches] + [res_shape] * (2 * ng),
        scratch_shapes=[pltpu.VMEM((tm, D_MODEL), BF16)],
        compiler_params=pltpu.CompilerParams(dimension_semantics=("arbitrary", "arbitrary"),
                                             vmem_limit_bytes=FUSED_VMEM_LIMIT),
        name="ffn_ple_caches",
    )(x1, p, gf, wfi, wfo, gp, wpg, wpp, gfin, *([q_t] * ng), *news, *caches)


def _rope_tables(pos):
    half = ROT_DIM // 2
    inv_freq = ROPE_THETA ** (-jnp.arange(half, dtype=F32) / half)
    ang = pos.astype(F32)[:, None] * inv_freq[None, :]
    cos, sin = jnp.cos(ang), jnp.sin(ang)
    n = pos.shape[0]
    zeros = lambda w: jnp.zeros((n, w), F32)
    cos_h = jnp.concatenate([cos, cos, jnp.ones((n, HEAD_DIM - ROT_DIM), F32)], axis=-1)
    sa_h = jnp.concatenate([-sin, zeros(HEAD_DIM - half)], axis=-1)
    sb_h = jnp.concatenate([zeros(half), sin, zeros(HEAD_DIM - ROT_DIM)], axis=-1)
    return tuple(jnp.tile(t, (1, H_G)) for t in (cos_h, sa_h, sb_h))


def _position_major(t):
    b, _, length = t.shape
    return jnp.transpose(t.reshape(b, 2, H_G, HEAD_DIM, length), (0, 4, 1, 2, 3))[None]


def kernel(x_prompt, x_sample, state_conv, cache_win_a, cache_win_b, cache_win_c, p_prompt, p_sample,
           w_in, g_mix, w_dw, b_dw, ln_g, ln_b, w_conv_out, w_attn_out, w_o, g_ffn, w_ffn_in, w_ffn_out,
           g_ple, w_ple_gate, w_ple_proj, g_final):
    assert w_in.shape[0] == 1, "single-layer step"
    w_in_b = w_in[0].astype(BF16)
    wco = w_conv_out[0].astype(BF16)
    wao = w_attn_out[0].astype(BF16)
    wo = w_o[0].astype(BF16)
    wfi = w_ffn_in[0].astype(BF16)
    wfo = w_ffn_out[0].astype(BF16)
    wpg = w_ple_gate[0].astype(BF16)
    wpp = w_ple_proj[0].astype(BF16)
    gfin = g_final.reshape(1, D_MODEL)
    n_prompt = BATCH * SEQ
    ctx = CONV_WIDTH - 1

    xp = x_prompt.reshape(n_prompt, D_MODEL)
    tabs_p = _rope_tables(jnp.arange(SEQ, dtype=jnp.int32))
    c_slabs, q, kva, kvb, kvc, gates, tail_a, tail_b, tail_c, u_tail = _in_proj_prompt(
        xp, g_mix, w_in_b, *tabs_p, w_dw[0], b_dw)
    per_batch = lambda t: t.reshape(BATCH, SEQ, t.shape[-1])
    o = _attention(per_batch(q), per_batch(kva), per_batch(kvb), per_batch(kvc))
    x1 = _prompt_mix(c_slabs, xp, o.reshape(n_prompt, GROUP_WIDTH), gates, ln_g, ln_b, wco, wao, wo)
    new_conv_prompt = u_tail[None, :, CONV_HALO - ctx:, :]

    n_s = DEC_BATCH
    xs = x_sample.reshape(n_s, D_MODEL)
    tabs_s = tuple(jnp.broadcast_to(t, (n_s, GROUP_WIDTH))
                   for t in _rope_tables(jnp.full((1,), PAST_LEN, jnp.int32)))
    u_s, q_t, new_a, new_b, new_c, gates_s = _in_proj_sample(xs, g_mix, w_in_b, *tabs_s)
    c_s, new_conv_t = _sample_conv(jnp.transpose(state_conv[0], (1, 0, 2)), u_s, w_dw[0], b_dw)
    caches_t = [jnp.transpose(c[0], (0, 2, 3, 4, 1)) for c in (cache_win_a, cache_win_b, cache_win_c)]
    fused = _ffn_cache(x1, p_prompt.reshape(n_prompt, PLE_DIM), g_ffn, wfi, wfo, g_ple, wpg, wpp, gfin,
                       q_t, [new_a, new_b, new_c], caches_t)
    y_prompt = fused[0]
    new_wins = [jnp.transpose(t, (0, 4, 1, 2, 3))[None] for t in fused[1:1 + N_GROUPS]]
    ogs = fused[1 + N_GROUPS:1 + 2 * N_GROUPS]
    lgs = fused[1 + 2 * N_GROUPS:]
    x1_s = _sample_mix(c_s, xs, ogs, lgs, gates_s, ln_g, ln_b, wco, wao, wo)
    y_sample = _ffn(x1_s, p_sample.reshape(n_s, PLE_DIM), g_ffn, wfi, wfo, g_ple, wpg, wpp, gfin,
                    n_s).reshape(n_s, 1, D_MODEL)
    new_conv_sample = jnp.transpose(new_conv_t, (1, 0, 2))[None]

    return (y_prompt.reshape(BATCH, SEQ, D_MODEL), y_sample, new_conv_prompt, _position_major(tail_a), _position_major(tail_b),
            _position_major(tail_c), new_conv_sample, new_wins[0], new_wins[1], new_wins[2])
```

```python
import jax
import jax.numpy as jnp
from jax import lax
from jax.experimental import pallas as pl
from jax.experimental.pallas import tpu as pltpu

D_MODEL = 1024
BATCH = 8
SEQ = 2048
DEC_BATCH = 128
PAST_LEN = 8192
HEAD_DIM = 64
GROUPS = ((128, 1), (512, 4), (2048, 16))
H_G = 4
N_GROUPS = len(GROUPS)
GROUP_WIDTH = H_G * HEAD_DIM
ATTN_WIDTH = N_GROUPS * GROUP_WIDTH
ROT_DIM = HEAD_DIM // 4
ROPE_THETA = 500000.0
C_CONV = D_MODEL
CONV_WIDTH = 31
D_FF = 2816
PLE_DIM = 256
NORM_EPS = 1e-6
NK = 128
Q_OFF = 2 * C_CONV
K_OFF = Q_OFF + ATTN_WIDTH
V_OFF = K_OFF + ATTN_WIDTH
GATE_OFF = V_OFF + ATTN_WIDTH
IN_COLS = GATE_OFF + 2 * D_MODEL

F32 = jnp.float32
BF16 = jnp.bfloat16
LANES = 128
SUBLANES = 8
N_SLABS = C_CONV // LANES
MASKED = -jnp.inf
VMEM_LIMIT = 52 * 1024 * 1024


def _const_spec(shape):
    return pl.BlockSpec(shape, lambda *_: (0,) * len(shape), pipeline_mode=pl.Buffered(1))


def _params(n_axes):
    return pltpu.CompilerParams(dimension_semantics=("arbitrary",) * n_axes,
                                vmem_limit_bytes=VMEM_LIMIT)


def _rms(x, g):
    return x * lax.rsqrt(jnp.mean(x * x, axis=-1, keepdims=True) + NORM_EPS) * g


def _sigmoid(x):
    return 0.5 * jnp.tanh(0.5 * x) + 0.5


PROMPT_TM = 512
TILES_PER_SEQ = SEQ // PROMPT_TM
CONV_HALO = 32
CONV_PITCH = 4
CONV_GROUP = CONV_PITCH * SUBLANES
CONV_UNROLL = 1


def _projector(x_ref, g_ref, w_ref, cos_ref, sa_ref, sb_ref):
    hb = _rms(x_ref[...], g_ref[...]).astype(BF16)
    ch = GROUP_WIDTH

    def mm(c0):
        return jnp.dot(hb, w_ref[:, c0:c0 + ch], preferred_element_type=F32)

    def glu(c):
        return mm(c) * _sigmoid(mm(C_CONV + c))

    def gates(gate_ref, lo=0, hi=2 * D_MODEL // ch):
        for c in range(lo * ch, hi * ch, ch):
            gate_ref[:, c:c + ch] = _sigmoid(mm(GATE_OFF + c)).astype(BF16)

    def rope(z):
        return (z * cos_ref[...] + pltpu.roll(z, ch - ROT_DIM // 2, 1) * sa_ref[...]
                + pltpu.roll(z, ROT_DIM // 2, 1) * sb_ref[...])

    def qkv(g):
        c = g * ch
        return rope(mm(Q_OFF + c)) * (HEAD_DIM ** -0.5), rope(mm(K_OFF + c)), mm(V_OFF + c)

    return qkv, glu, gates


def _conv_slab(win_ref, lc, wdw_ref, bdw_ref, c_ref, n_rows):
    first = CONV_HALO - (CONV_WIDTH - 1)
    cols = slice(lc * LANES, (lc + 1) * LANES)
    n_acc = CONV_UNROLL * CONV_PITCH
    offs = [(i // CONV_PITCH) * CONV_GROUP + i % CONV_PITCH for i in range(n_acc)]
    bias = jnp.broadcast_to(bdw_ref[:, cols], (SUBLANES, LANES))
    for base in range(0, n_rows, CONV_UNROLL * CONV_GROUP):
        accs = [bias] * n_acc
        for k in range(CONV_WIDTH):
            wk = wdw_ref[k:k + 1, cols]
            for i in range(n_acc):
                tap = win_ref[lc, pl.ds(base + first + k + offs[i], SUBLANES, stride=CONV_PITCH), :]
                accs[i] = accs[i] + tap * wk
        for i in range(n_acc):
            c_ref[lc, pl.ds(base + offs[i], SUBLANES, stride=CONV_PITCH), :] = accs[i]


def _in_proj_prompt_kernel(x_ref, g_ref, w_ref, cos_ref, sa_ref, sb_ref, wdw_ref, bdw_ref,
                           c_ref, q_ref, kva_ref, kvb_ref, kvc_ref, gate_ref, ta_ref, tb_ref, tc_ref, ut_ref,
                           win_ref):
    tm = PROMPT_TM
    ch = GROUP_WIDTH

    @pl.when(pl.program_id(0) == 0)
    def _():
        win_ref[:, tm:tm + CONV_HALO, :] = jnp.zeros((N_SLABS, CONV_HALO, LANES), F32)

    qkv, glu, gates = _projector(x_ref, g_ref, w_ref, cos_ref, sa_ref, sb_ref)
    first_tile = pl.program_id(0) % TILES_PER_SEQ == 0
    def head_group(g):
        kv_ref, t_ref = ((kva_ref, ta_ref), (kvb_ref, tb_ref), (kvc_ref, tc_ref))[g]
        q, k, v = qkv(g)
        q_ref[:, g * ch:(g + 1) * ch] = q
        kv_ref[:, 0:ch] = k
        kv_ref[:, ch:2 * ch] = v
        n = t_ref.shape[1]
        t_ref[0:ch, :] = k[tm - n:, :].T
        t_ref[ch:2 * ch, :] = v[tm - n:, :].T

    def glu_chunk(i):
        c = i * ch
        u = glu(c)
        ut_ref[:, c:c + ch] = u[tm - CONV_HALO:, :]
        for j in range(ch // LANES):
            lc = c // LANES + j
            win_ref[lc, 0:CONV_HALO, :] = jnp.where(first_tile, 0.0, win_ref[lc, tm:tm + CONV_HALO, :])
            win_ref[lc, CONV_HALO:CONV_HALO + tm, :] = u[:, j * LANES:(j + 1) * LANES]

    n_gate = 2 * D_MODEL // ch
    after_slab = [
        lambda: glu_chunk(1), lambda: head_group(0), lambda: glu_chunk(2), lambda: head_group(1),
        lambda: glu_chunk(3), lambda: head_group(2),
        lambda: gates(gate_ref, 0, n_gate // 2), lambda: gates(gate_ref, n_gate // 2, n_gate)]
    glu_chunk(0)
    for lc in range(N_SLABS):
        _conv_slab(win_ref, lc, wdw_ref, bdw_ref, c_ref, tm)
        after_slab[lc]()


def _in_proj_prompt(x2d, g, w, cos, sa, sb, wdw, bdw):
    tm = PROMPT_TM
    rows = x2d.shape[0]
    assert GROUPS[1][0] == tm and GROUPS[2][0] == SEQ and GROUPS[0][0] <= tm
    row = lambda n: pl.BlockSpec((tm, n), lambda i: (i, 0))
    tab = pl.BlockSpec((tm, GROUP_WIDTH), lambda i: (i % TILES_PER_SEQ, 0))
    kv_w = 2 * GROUP_WIDTH
    tail = lambda n: pl.BlockSpec((None, kv_w, n), lambda i: (i // TILES_PER_SEQ, 0, 0))
    return pl.pallas_call(
        _in_proj_prompt_kernel,
        grid=(rows // tm,),
        in_specs=[row(D_MODEL), _const_spec((1, D_MODEL)), _const_spec((D_MODEL, IN_COLS)), tab, tab, tab,
                  _const_spec((CONV_WIDTH, C_CONV)), _const_spec((1, C_CONV))],
        out_specs=[pl.BlockSpec((N_SLABS, tm, LANES), lambda i: (0, i, 0)),
                   row(ATTN_WIDTH), row(kv_w), row(kv_w), row(kv_w), row(2 * D_MODEL),
                   tail(GROUPS[0][0]), tail(GROUPS[1][0]),
                   pl.BlockSpec((None, kv_w, tm), lambda i: (i // TILES_PER_SEQ, 0, i % TILES_PER_SEQ)),
                   pl.BlockSpec((None, CONV_HALO, C_CONV), lambda i: (i // TILES_PER_SEQ, 0, 0))],
        out_shape=[jax.ShapeDtypeStruct((N_SLABS, rows, LANES), F32),
                   jax.ShapeDtypeStruct((rows, ATTN_WIDTH), F32),
                   jax.ShapeDtypeStruct((rows, kv_w), F32),
                   jax.ShapeDtypeStruct((rows, kv_w), F32),
                   jax.ShapeDtypeStruct((rows, kv_w), F32),
                   jax.ShapeDtypeStruct((rows, 2 * D_MODEL), BF16),
                   jax.ShapeDtypeStruct((BATCH, kv_w, GROUPS[0][0]), F32),
                   jax.ShapeDtypeStruct((BATCH, kv_w, GROUPS[1][0]), F32),
                   jax.ShapeDtypeStruct((BATCH, kv_w, SEQ), F32),
                   jax.ShapeDtypeStruct((BATCH, CONV_HALO, C_CONV), F32)],
        scratch_shapes=[pltpu.VMEM((N_SLABS, CONV_HALO + tm, LANES), F32)],
        compiler_params=_params(1),
        name="in_proj_prompt",
    )(x2d, g, w, cos, sa, sb, wdw, bdw)


def _in_proj_sample_kernel(x_ref, g_ref, w_ref, cos_ref, sa_ref, sb_ref,
                           u_ref, qt_ref, ta_ref, tb_ref, tc_ref, gate_ref):
    qkv, glu, gates = _projector(x_ref, g_ref, w_ref, cos_ref, sa_ref, sb_ref)
    ch = GROUP_WIDTH
    for g, t_ref in enumerate((ta_ref, tb_ref, tc_ref)):
        q, k, v = qkv(g)
        qt_ref[g * ch:(g + 1) * ch, :] = q.T
        t_ref[0:ch, :] = k.T
        t_ref[ch:2 * ch, :] = v.T
    for c in range(0, C_CONV, ch):
        u = glu(c)
        for j in range(ch // LANES):
            u_ref[c // LANES + j] = u[:, j * LANES:(j + 1) * LANES]
    gates(gate_ref)


def _in_proj_sample(x2d, g, w, cos, sa, sb):
    n = x2d.shape[0]
    full = lambda r, c: pl.BlockSpec((r, c), lambda i: (0, 0))
    kv_w = 2 * GROUP_WIDTH
    return pl.pallas_call(
        _in_proj_sample_kernel,
        grid=(1,),
        in_specs=[full(n, D_MODEL), _const_spec((1, D_MODEL)), _const_spec((D_MODEL, IN_COLS)),
                  full(n, GROUP_WIDTH), full(n, GROUP_WIDTH), full(n, GROUP_WIDTH)],
        out_specs=[pl.BlockSpec((N_SLABS, n, LANES), lambda i: (0, 0, 0)),
                   full(ATTN_WIDTH, n), full(kv_w, n), full(kv_w, n), full(kv_w, n), full(n, 2 * D_MODEL)],
        out_shape=[jax.ShapeDtypeStruct((N_SLABS, n, LANES), F32),
                   jax.ShapeDtypeStruct((ATTN_WIDTH, n), F32),
                   jax.ShapeDtypeStruct((kv_w, n), F32),
                   jax.ShapeDtypeStruct((kv_w, n), F32),
                   jax.ShapeDtypeStruct((kv_w, n), F32),
                   jax.ShapeDtypeStruct((n, 2 * D_MODEL), BF16)],
        compiler_params=_params(1),
        name="in_proj_sample",
    )(x2d, g, w, cos, sa, sb)


ATTN_UNROLL = 8


def _combine_groups(lses, outs):
    mx = jnp.maximum(jnp.maximum(lses[0], lses[1]), lses[2])
    es = [jnp.exp(l - mx) for l in lses]
    num = es[0] * outs[0] + es[1] * outs[1] + es[2] * outs[2]
    return num / (es[0] + es[1] + es[2])


def _largest_divisor(n, cap):
    return max(d for d in range(1, cap + 1) if n % d == 0)


def _attn_kernel(qa_ref, qb_ref, qc_ref, ka_ref, va_ref, kb_ref, vb_ref, kc_ref, vc_ref,
                 o_ref, og_ref, lg_ref):
    lane = lax.broadcasted_iota(jnp.int32, (NK, LANES), 1)
    qi = lax.broadcasted_iota(jnp.int32, (NK, LANES), 0)
    first_head = lane < HEAD_DIM
    own_mask = lane <= qi
    prev_mask = lane >= qi
    nt = (((1,), (1,)), ((), ()))

    def blocks(g, q_ref, k_ref, v_ref, row_list):
        scores, values = [], []
        for rows, prev_rows in row_list:
            qb = q_ref[rows, :]
            q2 = jnp.concatenate([jnp.where(first_head, qb, 0.0), jnp.where(first_head, 0.0, qb)],
                                 axis=0).astype(BF16)
            keys = k_ref[rows, :].astype(BF16)
            vals = v_ref[rows, :].astype(BF16)
            mask = own_mask
            if prev_rows is not None:
                keys = jnp.concatenate([keys, k_ref[prev_rows, :].astype(BF16)], axis=0)
                vals = jnp.concatenate([vals, v_ref[prev_rows, :].astype(BF16)], axis=0)
                mask = jnp.concatenate([own_mask, prev_mask], axis=1)
            values.append(jnp.concatenate([vals, jnp.ones_like(vals)], axis=1))
            s = lax.dot_general(q2, keys, nt, preferred_element_type=F32)
            scores.append(jnp.where(jnp.concatenate([mask, mask], axis=0), s, MASKED))
        probs, maxes = [], []
        for s in scores:
            m = jnp.max(s, axis=-1, keepdims=True)
            probs.append(jnp.exp(s - m).astype(BF16))
            maxes.append(m)
        for b, (rows, _) in enumerate(row_list):
            r = jnp.dot(probs[b], values[b], preferred_element_type=F32)
            l = r[:, LANES:]
            out = r[:, :LANES] / l
            lse = maxes[b] + jnp.log(l)
            og_ref[g, rows, :] = jnp.where(first_head, out[:NK], out[NK:])
            lg_ref[g, rows, :] = jnp.where(first_head, lse[:NK], lse[NK:])

    for g, (q_ref, k_ref, v_ref) in enumerate(((qa_ref, ka_ref, va_ref), (qb_ref, kb_ref, vb_ref),
                                               (qc_ref, kc_ref, vc_ref))):
        dil = GROUPS[g][1]
        span = NK * dil
        n_later = dil * (SEQ // span - 1)

        def rows_at(start, dil=dil):
            return pl.ds(start, NK) if dil == 1 else pl.ds(start, NK, stride=dil)

        n_first = _largest_divisor(dil, ATTN_UNROLL)

        def first(t, carry, g=g, q_ref=q_ref, k_ref=k_ref, v_ref=v_ref, rows_at=rows_at, n=n_first):
            blocks(g, q_ref, k_ref, v_ref, [(rows_at(t * n + j), None) for j in range(n)])
            return carry

        lax.fori_loop(0, dil // n_first, first, 0)
        if n_later:
            n_per = _largest_divisor(n_later, ATTN_UNROLL)

            def later(t, carry, g=g, q_ref=q_ref, k_ref=k_ref, v_ref=v_ref, rows_at=rows_at, dil=dil,
                      span=span, n=n_per):
                row_list = []
                for j in range(n):
                    i = t * n + j
                    start = i % dil + (1 + i // dil) * span
                    row_list.append((rows_at(start), rows_at(start - span)))
                blocks(g, q_ref, k_ref, v_ref, row_list)
                return carry

            lax.fori_loop(0, n_later // n_per, later, 0)

    tc = 256
    for t in range(0, SEQ, tc):
        rows = slice(t, t + tc)
        o_ref[rows, :] = _combine_groups([lg_ref[g, rows, :] for g in range(N_GROUPS)],
                                         [og_ref[g, rows, :] for g in range(N_GROUPS)]).astype(o_ref.dtype)


def _attention(q3, kva3, kvb3, kvc3):
    def col(c):
        return pl.BlockSpec((None, SEQ, LANES), lambda b, hp, c=c: (b, 0, c + hp))
    n_pairs = GROUP_WIDTH // LANES
    return pl.pallas_call(
        _attn_kernel,
        grid=(BATCH, n_pairs),
        in_specs=[col(0), col(n_pairs), col(2 * n_pairs),
                  col(0), col(n_pairs), col(0), col(n_pairs), col(0), col(n_pairs)],
        out_specs=pl.BlockSpec((None, SEQ, LANES), lambda b, hp: (b, 0, hp)),
        out_shape=jax.ShapeDtypeStruct((BATCH, SEQ, GROUP_WIDTH), BF16),
        scratch_shapes=[pltpu.VMEM((N_GROUPS, SEQ, LANES), F32), pltpu.VMEM((N_GROUPS, SEQ, LANES), F32)],
        compiler_params=_params(2),
        name="prompt_attention",
    )(q3, q3, q3, kva3, kva3, kvb3, kvb3, kvc3, kvc3)


def _merge(c, x, ob, gate_ref, lng_ref, lnb_ref, wco_ref, wao_ref, wo_ref):
    mu = jnp.mean(c, axis=-1, keepdims=True)
    d = c - mu
    var = jnp.mean(d * d, axis=-1, keepdims=True)
    y = d * lax.rsqrt(var + NORM_EPS) * lng_ref[...] + lnb_ref[...]
    a_out = jnp.dot(jax.nn.silu(y).astype(BF16), wco_ref[...], preferred_element_type=F32)
    b_out = jnp.dot(ob, wao_ref[...], preferred_element_type=F32)
    merged = (gate_ref[:, 0:D_MODEL].astype(F32) * a_out
              + gate_ref[:, D_MODEL:2 * D_MODEL].astype(F32) * b_out)
    return x + jnp.dot(merged.astype(BF16), wo_ref[...], preferred_element_type=F32)


MIX_TM = 1024


def _prompt_mix_kernel(c_ref, x_ref, o_ref, gate_ref, lng_ref, lnb_ref, wco_ref, wao_ref, wo_ref, x1_ref):
    c = jnp.concatenate([c_ref[lc] for lc in range(N_SLABS)], axis=-1)
    x1_ref[...] = _merge(c, x_ref[...], o_ref[...], gate_ref, lng_ref, lnb_ref, wco_ref, wao_ref, wo_ref)


def _prompt_mix(c_slabs, x2d, o2d, gate2d, lng, lnb, wco, wao, wo):
    tm = MIX_TM
    rows = x2d.shape[0]
    row = lambda n: pl.BlockSpec((tm, n), lambda i: (i, 0))
    return pl.pallas_call(
        _prompt_mix_kernel,
        grid=(rows // tm,),
        in_specs=[pl.BlockSpec((N_SLABS, tm, LANES), lambda i: (0, i, 0)),
                  row(D_MODEL), row(GROUP_WIDTH), row(2 * D_MODEL),
                  _const_spec((1, C_CONV)), _const_spec((1, C_CONV)),
                  _const_spec((C_CONV, D_MODEL)), _const_spec((GROUP_WIDTH, D_MODEL)),
                  _const_spec((D_MODEL, D_MODEL))],
        out_specs=row(D_MODEL),
        out_shape=jax.ShapeDtypeStruct((rows, D_MODEL), F32),
        compiler_params=_params(1),
        name="prompt_mix",
    )(c_slabs, x2d, o2d, gate2d, lng, lnb, wco, wao, wo)


def _sample_mix_kernel(c_ref, x_ref, oa_ref, ob_ref, oc_ref, la_ref, lb_ref, lc_ref, gate_ref,
                       lng_ref, lnb_ref, wco_ref, wao_ref, wo_ref, x1_ref):
    ob = _combine_groups([r[...].T for r in (la_ref, lb_ref, lc_ref)],
                         [r[...].T for r in (oa_ref, ob_ref, oc_ref)]).astype(BF16)
    x1_ref[...] = _merge(c_ref[...], x_ref[...], ob, gate_ref, lng_ref, lnb_ref, wco_ref, wao_ref, wo_ref)


def _sample_mix(c, x, ogs, lgs, gate, lng, lnb, wco, wao, wo):
    rows = x.shape[0]
    full = lambda n: _const_spec((rows, n))
    per_group = _const_spec((GROUP_WIDTH, rows))
    return pl.pallas_call(
        _sample_mix_kernel,
        grid=(1,),
        in_specs=[full(C_CONV), full(D_MODEL)] + [per_group] * (2 * N_GROUPS) + [full(2 * D_MODEL),
                  _const_spec((1, C_CONV)), _const_spec((1, C_CONV)),
                  _const_spec((C_CONV, D_MODEL)), _const_spec((GROUP_WIDTH, D_MODEL)),
                  _const_spec((D_MODEL, D_MODEL))],
        out_specs=pl.BlockSpec((rows, D_MODEL), lambda i: (0, 0)),
        out_shape=jax.ShapeDtypeStruct((rows, D_MODEL), F32),
        compiler_params=_params(1),
        name="sample_mix",
    )(c, x, *ogs, *lgs, gate, lng, lnb, wco, wao, wo)


FF_CHUNK = 256


def _ffn_chunks(hb, wfi_ref, wfo_ref, acc_ref, lo, hi):
    acts = []
    for c in range(lo * FF_CHUNK, hi * FF_CHUNK, FF_CHUNK):
        gch = jnp.dot(hb, wfi_ref[:, c:c + FF_CHUNK], preferred_element_type=F32)
        uch = jnp.dot(hb, wfi_ref[:, D_FF + c:D_FF + c + FF_CHUNK], preferred_element_type=F32)
        acts.append((gch * _sigmoid(gch) * uch).astype(BF16))
    act = jnp.concatenate(acts, axis=1) if len(acts) > 1 else acts[0]
    acc_ref[...] += jnp.dot(act, wfo_ref[lo * FF_CHUNK:hi * FF_CHUNK, :], preferred_element_type=F32)


def _ffn_finish(x1, ffn_out, p_ref, gp_ref, wpg_ref, wpp_ref, gfin_ref, y_ref):
    x2 = x1 + ffn_out
    hp = _rms(x2, gp_ref[...]).astype(BF16)
    gate = _sigmoid(jnp.dot(hp, wpg_ref[...], preferred_element_type=F32))
    pe = jnp.dot(p_ref[...].astype(BF16), wpp_ref[...], preferred_element_type=F32)
    x3 = x2 + gate * pe
    y_ref[...] = _rms(x3, gfin_ref[...])


def _ffn_kernel(x_ref, p_ref, gf_ref, wfi_ref, wfo_ref, gp_ref, wpg_ref, wpp_ref, gfin_ref, y_ref, acc_ref):
    x1 = x_ref[...]
    hb = _rms(x1, gf_ref[...]).astype(BF16)
    acc_ref[...] = jnp.zeros_like(acc_ref)
    _ffn_chunks(hb, wfi_ref, wfo_ref, acc_ref, 0, D_FF // FF_CHUNK)
    _ffn_finish(x1, acc_ref[...], p_ref, gp_ref, wpg_ref, wpp_ref, gfin_ref, y_ref)


def _ffn(x1, p, gf, wfi, wfo, gp, wpg, wpp, gfin, tm):
    rows = x1.shape[0]
    row = lambda n: pl.BlockSpec((tm, n), lambda i: (i, 0))
    return pl.pallas_call(
        _ffn_kernel,
        grid=(rows // tm,),
        in_specs=[row(D_MODEL), row(PLE_DIM), _const_spec((1, D_MODEL)),
                  _const_spec((D_MODEL, 2 * D_FF)), _const_spec((D_FF, D_MODEL)),
                  _const_spec((1, D_MODEL)), _const_spec((D_MODEL, D_MODEL)),
                  _const_spec((PLE_DIM, D_MODEL)), _const_spec((1, D_MODEL))],
        out_specs=row(D_MODEL),
        out_shape=jax.ShapeDtypeStruct((rows, D_MODEL), F32),
        scratch_shapes=[pltpu.VMEM((tm, D_MODEL), F32)],
        compiler_params=_params(1),
        name="ffn_ple",
    )(x1, p, gf, wfi, wfo, gp, wpg, wpp, gfin)


SAMPLE_CONV_BLOCK = 32


def _sample_conv_kernel(state_ref, u_ref, wdw_ref, bdw_ref, c_ref, new_ref):
    ctx = CONV_WIDTH - 1
    for lc in range(N_SLABS):
        cols = slice(lc * LANES, (lc + 1) * LANES)
        u = u_ref[lc]
        acc = u * wdw_ref[ctx:ctx + 1, cols] + bdw_ref[:, cols]
        for k in range(ctx):
            acc = acc + state_ref[k, :, cols] * wdw_ref[k:k + 1, cols]
        c_ref[:, cols] = acc
        new_ref[ctx - 1, :, cols] = u
    for k in range(ctx - 1):
        new_ref[k] = state_ref[k + 1]


def _sample_conv(state_t, u_slabs, wdw, bdw):
    ctx, n, _ = state_t.shape
    sb = SAMPLE_CONV_BLOCK
    return pl.pallas_call(
        _sample_conv_kernel,
        grid=(n // sb,),
        in_specs=[pl.BlockSpec((ctx, sb, C_CONV), lambda i: (0, i, 0)),
                  pl.BlockSpec((N_SLABS, sb, LANES), lambda i: (0, i, 0)),
                  _const_spec((CONV_WIDTH, C_CONV)), _const_spec((1, C_CONV))],
        out_specs=[pl.BlockSpec((sb, C_CONV), lambda i: (i, 0)),
                   pl.BlockSpec((ctx, sb, C_CONV), lambda i: (0, i, 0))],
        out_shape=[jax.ShapeDtypeStruct((n, C_CONV), F32),
                   jax.ShapeDtypeStruct((ctx, n, C_CONV), F32)],
        compiler_params=_params(1),
        name="sample_conv",
    )(state_t, u_slabs, wdw, bdw)


def _cache_sample(q_ref, new_ref, cache_ref, out_ref, o_ref, lse_ref, s, sample, dil):
    length = cache_ref.shape[-1]
    pos = lax.broadcasted_iota(jnp.int32, (1, length), 1)
    used = (pos & (dil - 1)) == 0
    lane = lax.broadcasted_iota(jnp.int32, (HEAD_DIM, LANES), 1)
    last = lane == LANES - 1
    mine = lane == sample

    def pick(ref, r0):
        return jnp.sum(jnp.where(mine, ref[r0:r0 + HEAD_DIM, :], 0.0), axis=1, keepdims=True)

    def shift(kv, h, new):
        rolled = pltpu.roll(cache_ref[s, kv, h], length - 1, 1)
        out_ref[s, kv, h] = rolled
        out_ref[s, kv, h, :, length - LANES:] = jnp.where(last, new, rolled[:, length - LANES:])

    heads = range(H_G)
    q = [pick(q_ref, h * HEAD_DIM) for h in heads]
    k_new = [pick(new_ref, h * HEAD_DIM) for h in heads]
    v_new = [pick(new_ref, GROUP_WIDTH + h * HEAD_DIM) for h in heads]
    shift(0, 0, k_new[0])
    shift(1, 0, v_new[0])
    sc = [jnp.where(used, jnp.sum(cache_ref[s, 0, h] * q[h], axis=0, keepdims=True), MASKED) for h in heads]
    s_new = [jnp.sum(k_new[h] * q[h], axis=0, keepdims=True) for h in heads]
    m = [jnp.maximum(jnp.max(sc[h], axis=1, keepdims=True), s_new[h]) for h in heads]
    shift(0, 1, k_new[1])
    shift(1, 1, v_new[1])
    p = [jnp.exp(sc[h] - m[h]) for h in heads]
    p_new = [jnp.exp(s_new[h] - m[h]) for h in heads]
    l = [jnp.sum(p[h], axis=1, keepdims=True) + p_new[h] for h in heads]
    shift(0, 2, k_new[2])
    shift(1, 2, v_new[2])
    o = [(jnp.sum(cache_ref[s, 1, h] * p[h], axis=1, keepdims=True) + v_new[h] * p_new[h]) / l[h]
         for h in heads]
    shift(0, 3, k_new[3])
    shift(1, 3, v_new[3])
    for h in heads:
        rows = slice(h * HEAD_DIM, (h + 1) * HEAD_DIM)
        o_ref[rows, :] = jnp.where(mine, o[h], o_ref[rows, :])
        lse_ref[rows, :] = jnp.where(mine, m[h] + jnp.log(l[h]), lse_ref[rows, :])


FFN_PARTS = 4
FUSED_VMEM_LIMIT = 57 * 1024 * 1024


def _ffn_cache_kernel(x_ref, p_ref, gf_ref, wfi_ref, wfo_ref, gp_ref, wpg_ref, wpp_ref, gfin_ref, *refs):
    ng = N_GROUPS
    q_refs, new_refs, cache_refs = refs[0:ng], refs[ng:2 * ng], refs[2 * ng:3 * ng]
    y_ref = refs[3 * ng]
    out_refs, o_refs, lse_refs = refs[3 * ng + 1:4 * ng + 1], refs[4 * ng + 1:5 * ng + 1], refs[5 * ng + 1:6 * ng + 1]
    (hb_ref,) = refs[6 * ng + 1:]
    acc_ref = y_ref
    tile = pl.program_id(0)
    part = pl.program_id(1)

    @pl.when((tile == 0) & (part == 0))
    def _():
        for r in o_refs + lse_refs:
            r[...] = jnp.zeros_like(r)

    n_chunks = D_FF // FF_CHUNK
    bounds = [(n_chunks + 2) * j // FFN_PARTS for j in range(FFN_PARTS)] + [n_chunks]

    for j in range(FFN_PARTS):
        @pl.when(part == j)
        def _(j=j):
            if j == 0:
                hb_ref[...] = _rms(x_ref[...], gf_ref[...]).astype(BF16)
                acc_ref[...] = jnp.zeros_like(acc_ref)
            _ffn_chunks(hb_ref[...], wfi_ref, wfo_ref, acc_ref, bounds[j], bounds[j + 1])
            if j == FFN_PARTS - 1:
                _ffn_finish(x_ref[...], acc_ref[...], p_ref, gp_ref, wpg_ref, wpp_ref, gfin_ref, y_ref)
            for g in reversed(range(ng)):
                _cache_sample(q_refs[g], new_refs[g], cache_refs[g], out_refs[g], o_refs[g], lse_refs[g],
                              0, tile * FFN_PARTS + part, GROUPS[g][1])


def _ffn_cache(x1, p, gf, wfi, wfo, gp, wpg, wpp, gfin, q_t, news, caches):
    tm = PROMPT_TM
    rows = x1.shape[0]
    n = caches[0].shape[0]
    assert (rows // tm) * FFN_PARTS == n
    row = lambda w: pl.BlockSpec((tm, w), lambda i, k: (i, 0))
    bigs = [pl.BlockSpec((1,) + c.shape[1:], lambda i, k: (i * FFN_PARTS + k, 0, 0, 0, 0)) for c in caches]
    res = pl.BlockSpec((GROUP_WIDTH, n), lambda i, k: (0, 0))
    res_shape = jax.ShapeDtypeStruct((GROUP_WIDTH, n), F32)
    ng = N_GROUPS
    return pl.pallas_call(
        _ffn_cache_kernel,
        grid=(rows // tm, FFN_PARTS),
        in_specs=[row(D_MODEL), row(PLE_DIM), _const_spec((1, D_MODEL)),
                  _const_spec((D_MODEL, 2 * D_FF)), _const_spec((D_FF, D_MODEL)),
                  _const_spec((1, D_MODEL)), _const_spec((D_MODEL, D_MODEL)),
                  _const_spec((PLE_DIM, D_MODEL)), _const_spec((1, D_MODEL))]
                 + [pl.BlockSpec((GROUP_WIDTH, n), lambda i, k, g=g: (g, 0), pipeline_mode=pl.Buffered(1))
                    for g in range(ng)]
                 + [_const_spec((2 * GROUP_WIDTH, n))] * ng + bigs,
        out_specs=[row(D_MODEL)] + bigs + [res] * (2 * ng),
        out_shape=[jax.ShapeDtypeStruct((rows, D_MODEL), F32)]
                  + [jax.ShapeDtypeStruct(c.shape, F32) for c in caches] + [res_shape] * (2 * ng),
        scratch_shapes=[pltpu.VMEM((tm, D_MODEL), BF16)],
        compiler_params=pltpu.CompilerParams(dimension_semantics=("arbitrary", "arbitrary"),
                                             vmem_limit_bytes=FUSED_VMEM_LIMIT),
        name="ffn_ple_caches",
    )(x1, p, gf, wfi, wfo, gp, wpg, wpp, gfin, *([q_t] * ng), *news, *caches)


def _rope_tables(pos):
    half = ROT_DIM // 2
    inv_freq = ROPE_THETA ** (-jnp.arange(half, dtype=F32) / half)
    ang = pos.astype(F32)[:, None] * inv_freq[None, :]
    cos, sin = jnp.cos(ang), jnp.sin(ang)
    n = pos.shape[0]
    zeros = lambda w: jnp.zeros((n, w), F32)
    cos_h = jnp.concatenate([cos, cos, jnp.ones((n, HEAD_DIM - ROT_DIM), F32)], axis=-1)
    sa_h = jnp.concatenate([-sin, zeros(HEAD_DIM - half)], axis=-1)
    sb_h = jnp.concatenate([zeros(half), sin, zeros(HEAD_DIM - ROT_DIM)], axis=-1)
    return tuple(jnp.tile(t, (1, H_G)) for t in (cos_h, sa_h, sb_h))


def _position_major(t):
    b, _, length = t.shape
    return jnp.transpose(t.reshape(b, 2, H_G, HEAD_DIM, length), (0, 4, 1, 2, 3))[None]


def kernel(x_prompt, x_sample, state_conv, cache_win_a, cache_win_b, cache_win_c, p_prompt, p_sample,
           w_in, g_mix, w_dw, b_dw, ln_g, ln_b, w_conv_out, w_attn_out, w_o, g_ffn, w_ffn_in, w_ffn_out,
           g_ple, w_ple_gate, w_ple_proj, g_final):
    assert w_in.shape[0] == 1, "single-layer step"
    w_in_b = w_in[0].astype(BF16)
    wco = w_conv_out[0].astype(BF16)
    wao = w_attn_out[0].astype(BF16)
    wo = w_o[0].astype(BF16)
    wfi = w_ffn_in[0].astype(BF16)
    wfo = w_ffn_out[0].astype(BF16)
    wpg = w_ple_gate[0].astype(BF16)
    wpp = w_ple_proj[0].astype(BF16)
    gfin = g_final.reshape(1, D_MODEL)
    n_prompt = BATCH * SEQ
    ctx = CONV_WIDTH - 1

    xp = x_prompt.reshape(n_prompt, D_MODEL)
    tabs_p = _rope_tables(jnp.arange(SEQ, dtype=jnp.int32))
    c_slabs, q, kva, kvb, kvc, gates, tail_a, tail_b, tail_c, u_tail = _in_proj_prompt(
        xp, g_mix, w_in_b, *tabs_p, w_dw[0], b_dw)
    per_batch = lambda t: t.reshape(BATCH, SEQ, t.shape[-1])
    o = _attention(per_batch(q), per_batch(kva), per_batch(kvb), per_batch(kvc))
    x1 = _prompt_mix(c_slabs, xp, o.reshape(n_prompt, GROUP_WIDTH), gates, ln_g, ln_b, wco, wao, wo)
    new_conv_prompt = u_tail[None, :, CONV_HALO - ctx:, :]

    n_s = DEC_BATCH
    xs = x_sample.reshape(n_s, D_MODEL)
    tabs_s = tuple(jnp.broadcast_to(t, (n_s, GROUP_WIDTH))
                   for t in _rope_tables(jnp.full((1,), PAST_LEN, jnp.int32)))
    u_s, q_t, new_a, new_b, new_c, gates_s = _in_proj_sample(xs, g_mix, w_in_b, *tabs_s)
    c_s, new_conv_t = _sample_conv(jnp.transpose(state_conv[0], (1, 0, 2)), u_s, w_dw[0], b_dw)
    caches_t = [jnp.transpose(c[0], (0, 2, 3, 4, 1)) for c in (cache_win_a, cache_win_b, cache_win_c)]
    fused = _ffn_cache(x1, p_prompt.reshape(n_prompt, PLE_DIM), g_ffn, wfi, wfo, g_ple, wpg, wpp, gfin,
                       q_t, [new_a, new_b, new_c], caches_t)
    y_prompt = fused[0]
    new_wins = [jnp.transpose(t, (0, 4, 1, 2, 3))[None] for t in fused[1:1 + N_GROUPS]]
    ogs = fused[1 + N_GROUPS:1 + 2 * N_GROUPS]
    lgs = fused[1 + 2 * N_GROUPS:]
    x1_s = _sample_mix(c_s, xs, ogs, lgs, gates_s, ln_g, ln_b, wco, wao, wo)
    y_sample = _ffn(x1_s, p_sample.reshape(n_s, PLE_DIM), g_ffn, wfi, wfo, g_ple, wpg, wpp, gfin,
                    n_s).reshape(n_s, 1, D_MODEL)
    new_conv_sample = jnp.transpose(new_conv_t, (1, 0, 2))[None]

    return (y_prompt.reshape(BATCH, SEQ, D_MODEL), y_sample, new_conv_prompt, _position_major(tail_a), _position_major(tail_b),
            _position_major(tail_c), new_conv_sample, new_wins[0], new_wins[1], new_wins[2])
```
